```python
import math
import jax, jax.numpy as jnp
from jax import lax
import numpy as np

D_MODEL = 2048
BATCH = 4
SEQ = 2048
DEPTH = 2
DEC_BATCH = 128
DEC_SEQ = 4
PAST_LEN = 2048
PAGE_SIZE = 128

HEAD_DIM = 128
N_MEM_HEADS = 4
MEM_WIDTH = N_MEM_HEADS * HEAD_DIM
MIX_WIDTH = D_MODEL - MEM_WIDTH
N_HEADS = MIX_WIDTH // HEAD_DIM
N_GROUPS = MIX_WIDTH // HEAD_DIM
CHUNK = 128
D_FF = 4 * D_MODEL
N_MEM = 256
MOBA_BLOCK = 256
MOBA_TOPK = 3
ROPE_THETA = 500000.0
ROT_DIM = HEAD_DIM // 4
Q_BLOCK = 128
N_A = DEPTH // 2
N_B = DEPTH - N_A
EPS = 1e-6
SCALE = HEAD_DIM ** -0.5
F32 = jnp.float32
NEG_INF = -jnp.inf

kernel_name = 'yoco_gmlp_moba_memory_decoder_step'


def rmsnorm(x, g):
    xf = x.astype(F32)
    y = xf * lax.rsqrt(jnp.mean(xf * xf, axis=-1, keepdims=True) + EPS)
    return (y * g.astype(F32)).astype(x.dtype)


def rotary(x, pos):
    half = ROT_DIM // 2
    inv = ROPE_THETA ** (-(jnp.arange(half, dtype=F32) * 2.0) / ROT_DIM)
    ang = pos.astype(F32)[:, None] * inv[None, :]
    cos = jnp.cos(ang)[:, None, :]
    sin = jnp.sin(ang)[:, None, :]
    xf = x.astype(F32)
    x1 = xf[..., :half]
    x2 = xf[..., half:ROT_DIM]
    out = jnp.concatenate([x1 * cos - x2 * sin, x2 * cos + x1 * sin, xf[..., ROT_DIM:]], axis=-1)
    return out.astype(x.dtype)


def mem_kv(mem, g_mem, w_mem_kv):
    b = mem.shape[0]
    kv = rmsnorm(mem, g_mem) @ w_mem_kv
    k = kv[..., :MEM_WIDTH].reshape(b, N_MEM, N_MEM_HEADS, HEAD_DIM)
    v = kv[..., MEM_WIDTH:].reshape(b, N_MEM, N_MEM_HEADS, HEAD_DIM)
    return k, v


def mem_attend(q, k, v):
    b, t = q.shape[:2]
    s = jnp.einsum('bthd,bmhd->bhtm', q.astype(F32), k.astype(F32)) * SCALE
    p = jax.nn.softmax(s, axis=-1)
    o = jnp.einsum('bhtm,bmhd->bthd', p, v.astype(F32))
    return o.reshape(b, t, MEM_WIDTH).astype(q.dtype)


def chunk_gmlp(uv, g_v, w_s, b_s):
    u = uv[..., :MIX_WIDTH]
    v = rmsnorm(uv[..., MIX_WIDTH:], g_v)
    b, t = u.shape[:2]
    c = min(t, CHUNK)
    n = t // c
    vg = v.reshape(b, n, c, N_GROUPS, HEAD_DIM)
    ws = jnp.where(jnp.tril(jnp.ones((c, c), bool))[None], w_s[:, :c, :c], 0.0).astype(v.dtype)
    mixed = jnp.einsum('gts,bnsgc->bntgc', ws, vg) + b_s[:, :c].T[None, None, :, :, None]
    return u * mixed.reshape(b, t, MIX_WIDTH), v


def block_pack(k, v):
    b, s = k.shape[:2]
    nb = -(-s // MOBA_BLOCK)
    widths = ((0, 0), (0, nb * MOBA_BLOCK - s), (0, 0), (0, 0))
    kb = jnp.pad(k, widths).reshape(b, nb, MOBA_BLOCK, N_HEADS, HEAD_DIM)
    vb = jnp.pad(v, widths).reshape(b, nb, MOBA_BLOCK, N_HEADS, HEAD_DIM)
    kmean = jnp.mean(kb.astype(F32), axis=2)
    return kb, vb, kmean


def moba_prompt(q, kb, vb, kmean):
    b_sz, s = q.shape[:2]
    nb = kb.shape[1]
    nq = s // Q_BLOCK
    n_sel = min(MOBA_TOPK, nb - 1)
    hid = jnp.arange(N_HEADS)[None, :, None]
    blk_ids = jnp.arange(nb)

    def one(args):
        qs, b, i = args
        qf = qs.astype(F32)
        q0 = i * Q_BLOCK
        c = q0 // MOBA_BLOCK
        qpos = q0 + jnp.arange(Q_BLOCK)
        kpos = c * MOBA_BLOCK + jnp.arange(MOBA_BLOCK)
        kc = kb[b, c].astype(F32)
        vc = vb[b, c].astype(F32)
        s_cur = jnp.einsum('qhd,khd->hqk', qf, kc) * SCALE
        s_cur = jnp.where((kpos[None, :] <= qpos[:, None])[None], s_cur, NEG_INF)
        if n_sel == 0:
            p = jax.nn.softmax(s_cur, axis=-1)
            return jnp.einsum('hqk,khd->qhd', p, vc)
        gate = jnp.einsum('qhd,nhd->qhn', qf, kmean[b])
        gate = jnp.where(blk_ids < c, gate, NEG_INF)
        _, sel = lax.top_k(gate, n_sel)
        valid = (sel < c).transpose(1, 0, 2)
        n_k = n_sel * MOBA_BLOCK
        ks = kb[b, sel, :, hid].reshape(Q_BLOCK, N_HEADS, n_k, HEAD_DIM).astype(F32)
        vs = vb[b, sel, :, hid].reshape(Q_BLOCK, N_HEADS, n_k, HEAD_DIM).astype(F32)
        s_sel = jnp.einsum('qhd,qhkd->hqk', qf, ks) * SCALE
        s_sel = jnp.where(jnp.repeat(valid, MOBA_BLOCK, axis=-1), s_sel, NEG_INF)
        p = jax.nn.softmax(jnp.concatenate([s_sel, s_cur], axis=-1), axis=-1)
        return (jnp.einsum('hqk,qhkd->qhd', p[..., :n_k], vs)
                + jnp.einsum('hqk,khd->qhd', p[..., n_k:], vc))

    qblocks = q.reshape(b_sz * nq, Q_BLOCK, N_HEADS, HEAD_DIM)
    flat = jnp.arange(b_sz * nq, dtype=jnp.int32)
    out = lax.map(one, (qblocks, flat // nq, flat % nq))
    return out.reshape(b_sz, s, MIX_WIDTH).astype(q.dtype)


def sample_block_means(k_pool, page_table):
    c = PAST_LEN // MOBA_BLOCK
    ppb = MOBA_BLOCK // PAGE_SIZE
    if c == 0:
        return None
    return lax.map(lambda pt: k_pool[pt[:c * ppb]].astype(F32)
                   .reshape(c, MOBA_BLOCK, N_HEADS, HEAD_DIM).mean(axis=1), page_table)


def moba_sample(q, k_new, v_new, kmean, k_pool, v_pool, page_table):
    t = q.shape[1]
    c = PAST_LEN // MOBA_BLOCK
    ppb = MOBA_BLOCK // PAGE_SIZE
    n_pages = PAST_LEN // PAGE_SIZE
    first_cur = c * ppb
    n_r = (n_pages - first_cur) * PAGE_SIZE
    n_sel = min(MOBA_TOPK, c)
    n_k = n_sel * MOBA_BLOCK
    hid = jnp.arange(N_HEADS)[None, :, None, None]
    mask_c = jnp.concatenate([jnp.ones((t, n_r), bool), jnp.tril(jnp.ones((t, t), bool))], axis=1)

    def one(args):
        qs, kn, vn, pt = args[0], args[1], args[2], args[3]
        qf = qs.astype(F32)
        cur_pages = pt[first_cur:]
        kc = jnp.concatenate([k_pool[cur_pages].reshape(n_r, N_HEADS, HEAD_DIM), kn], axis=0).astype(F32)
        vc = jnp.concatenate([v_pool[cur_pages].reshape(n_r, N_HEADS, HEAD_DIM), vn], axis=0).astype(F32)
        s_cur = jnp.einsum('thd,khd->htk', qf, kc) * SCALE
        s_cur = jnp.where(mask_c[None], s_cur, NEG_INF)
        if n_sel == 0:
            p = jax.nn.softmax(s_cur, axis=-1)
            return jnp.einsum('htk,khd->thd', p, vc)
        km = args[4]
        gate = jnp.einsum('thd,nhd->thn', qf, km)
        _, sel = lax.top_k(gate, n_sel)
        pages = pt[sel[..., None] * ppb + jnp.arange(ppb)]
        ks = k_pool[pages, :, hid].reshape(t, N_HEADS, n_k, HEAD_DIM).astype(F32)
        vs = v_pool[pages, :, hid].reshape(t, N_HEADS, n_k, HEAD_DIM).astype(F32)
        s_sel = jnp.einsum('thd,thkd->htk', qf, ks) * SCALE
        p = jax.nn.softmax(jnp.concatenate([s_sel, s_cur], axis=-1), axis=-1)
        return (jnp.einsum('htk,thkd->thd', p[..., :n_k], vs)
                + jnp.einsum('htk,khd->thd', p[..., n_k:], vc))

    xs = (q, k_new, v_new, page_table) + ((kmean,) if n_sel > 0 else ())
    out = lax.map(one, xs)
    return out.reshape(q.shape[0], t, MIX_WIDTH).astype(q.dtype)


def setup_inputs(seed: int = 0) -> dict:
    key = jax.random.key(seed)
    ks = jax.random.split(key, 24)
    n_pages = PAST_LEN // PAGE_SIZE
    n_used = DEC_BATCH * n_pages
    n_phys = (n_used * 5 + 3) // 4

    def nrm(k, shape, scale=1.0):
        return jax.random.normal(k, shape, F32) * scale

    def gain(k, shape):
        return 1.0 + 0.02 * jax.random.normal(k, shape, F32)

    page_table = jax.random.permutation(ks[6], n_phys)[:n_used].reshape(DEC_BATCH, n_pages).astype(jnp.int32)
    return {
        'x_prompt': nrm(ks[0], (BATCH, SEQ, D_MODEL)),
        'x_sample': nrm(ks[1], (DEC_BATCH, DEC_SEQ, D_MODEL)),
        'cache_k': nrm(ks[2], (n_phys, PAGE_SIZE, N_HEADS, HEAD_DIM)),
        'cache_v': nrm(ks[3], (n_phys, PAGE_SIZE, N_HEADS, HEAD_DIM)),
        'cache_mem_k': nrm(ks[4], (DEPTH, DEC_BATCH, N_MEM, N_MEM_HEADS, HEAD_DIM)),
        'cache_mem_v': nrm(ks[5], (DEPTH, DEC_BATCH, N_MEM, N_MEM_HEADS, HEAD_DIM)),
        'page_table': page_table,
        'mem_prompt': nrm(ks[7], (BATCH, N_MEM, D_MODEL)),
        'g_mix': gain(ks[8], (DEPTH, D_MODEL)),
        'w_in_a': nrm(ks[9], (N_A, D_MODEL, 2 * MIX_WIDTH + MEM_WIDTH), D_MODEL ** -0.5),
        'w_in_b': nrm(ks[10], (N_B, D_MODEL, MIX_WIDTH + MEM_WIDTH), D_MODEL ** -0.5),
        'g_v': gain(ks[11], (N_A, MIX_WIDTH)),
        'w_s': nrm(ks[12], (N_A, N_GROUPS, CHUNK, CHUNK), CHUNK ** -0.5),
        'b_s': gain(ks[13], (N_A, N_GROUPS, CHUNK)),
        'w_out': nrm(ks[14], (DEPTH, MIX_WIDTH + MEM_WIDTH, D_MODEL), (MIX_WIDTH + MEM_WIDTH) ** -0.5),
        'g_mlp': gain(ks[15], (DEPTH, D_MODEL)),
        'w_up': nrm(ks[16], (DEPTH, D_MODEL, D_FF), D_MODEL ** -0.5),
        'w_down': nrm(ks[17], (DEPTH, D_FF, D_MODEL), D_FF ** -0.5),
        'g_mem': gain(ks[18], (DEPTH, D_MODEL)),
        'w_mem_kv': nrm(ks[19], (DEPTH, D_MODEL, 2 * MEM_WIDTH), D_MODEL ** -0.5),
        'g_kv': gain(ks[20], (D_MODEL,)),
        'w_kv': nrm(ks[21], (D_MODEL, 2 * MIX_WIDTH), D_MODEL ** -0.5),
        'g_final': gain(ks[22], (D_MODEL,)),
    }


def reference(x_prompt, x_sample, cache_k, cache_v, cache_mem_k, cache_mem_v, page_table,
              mem_prompt, g_mix, w_in_a, w_in_b, g_v, w_s, b_s, w_out, g_mlp, w_up, w_down,
              g_mem, w_mem_kv, g_kv, w_kv, g_final):
    pos_p = jnp.arange(SEQ, dtype=jnp.int32)
    pos_s = PAST_LEN + jnp.arange(DEC_SEQ, dtype=jnp.int32)

    mk_list, mv_list = [], []
    for l in range(DEPTH):
        mk, mv = mem_kv(mem_prompt, g_mem[l], w_mem_kv[l])
        mk_list.append(mk)
        mv_list.append(mv)
    mem_k_prompt = jnp.stack(mk_list)
    mem_v_prompt = jnp.stack(mv_list)

    def project_kv(x, pos):
        lead = x.shape[:2]
        kv = rmsnorm(x, g_kv) @ w_kv
        k = rotary(kv[..., :MIX_WIDTH].reshape(*lead, N_HEADS, HEAD_DIM), pos)
        v = kv[..., MIX_WIDTH:].reshape(*lead, N_HEADS, HEAD_DIM)
        return k, v

    def shared_prompt(x):
        k, v = project_kv(x, pos_p)
        kb, vb, kmean = block_pack(k, v)
        return (k, v, kb, vb, kmean)

    def shared_sample(x):
        k, v = project_kv(x, pos_s)
        return (k, v, sample_block_means(cache_k, page_table))

    def trunk(x, pos, mem_k, mem_v, make_shared, moba):
        lead = x.shape[:2]
        v_rows = []
        shared = None
        for l in range(DEPTH):
            h = rmsnorm(x, g_mix[l])
            if l < N_A:
                z = h @ w_in_a[l]
                mix, v_l = chunk_gmlp(jax.nn.gelu(z[..., :2 * MIX_WIDTH]), g_v[l], w_s[l], b_s[l])
                v_rows.append(v_l)
                qm = z[..., 2 * MIX_WIDTH:]
            else:
                if shared is None:
                    shared = make_shared(x)
                z = h @ w_in_b[l - N_A]
                q = rotary(z[..., :MIX_WIDTH].reshape(*lead, N_HEADS, HEAD_DIM), pos)
                mix = moba(q, shared)
                qm = z[..., MIX_WIDTH:]
            mo = mem_attend(qm.reshape(*lead, N_MEM_HEADS, HEAD_DIM), mem_k[l], mem_v[l])
            x = x + jnp.concatenate([mix, mo], axis=-1) @ w_out[l]
            a = jax.nn.relu(rmsnorm(x, g_mlp[l]) @ w_up[l])
            x = x + (a * a) @ w_down[l]
        return rmsnorm(x, g_final), v_rows, shared

    y_prompt, _, sh_p = trunk(x_prompt, pos_p, mem_k_prompt, mem_v_prompt, shared_prompt,
                              lambda q, sh: moba_prompt(q, sh[2], sh[3], sh[4]))
    y_sample, v_rows_s, sh_s = trunk(x_sample, pos_s, cache_mem_k, cache_mem_v, shared_sample,
                                     lambda q, sh: moba_sample(q, sh[0], sh[1], sh[2], cache_k, cache_v, page_table))
    k_prompt, v_prompt = sh_p[0], sh_p[1]
    k_sample, v_sample = sh_s[0], sh_s[1]
    gmlp_v_sample = jnp.stack(v_rows_s)
    return (y_prompt, y_sample, k_prompt, v_prompt, k_sample, v_sample, mem_k_prompt, mem_v_prompt, gmlp_v_sample)
```

```python
import functools
import math

import jax
import jax.numpy as jnp
from jax import lax
from jax.experimental import pallas as pl
from jax.experimental.pallas import tpu as pltpu

F32 = jnp.float32
BF16 = jnp.bfloat16

D_MODEL = 2048
BATCH = 4
SEQ = 2048
DEC_BATCH = 128
DEC_SEQ = 4
PAST_LEN = 2048
PAGE_SIZE = 128
HEAD_DIM = 128
N_MEM_HEADS = 4
MEM_WIDTH = N_MEM_HEADS * HEAD_DIM
MIX_WIDTH = D_MODEL - MEM_WIDTH
N_HEADS = MIX_WIDTH // HEAD_DIM
N_GROUPS = MIX_WIDTH // HEAD_DIM
CHUNK = 128
D_FF = 4 * D_MODEL
N_MEM = 256
MOBA_BLOCK = 256
MOBA_TOPK = 3
ROPE_THETA = 500000.0
ROT_DIM = HEAD_DIM // 4
EPS = 1e-6
SCALE = HEAD_DIM ** -0.5

M_PROMPT = BATCH * SEQ
M_SAMPLE = DEC_BATCH * DEC_SEQ
M_ALL = M_PROMPT + M_SAMPLE
TM = M_SAMPLE
N_PROMPT_TILES = M_PROMPT // TM
N_BLOCKS = SEQ // MOBA_BLOCK
N_PAST_BLOCKS = PAST_LEN // MOBA_BLOCK
PAGES_PER_BLOCK = MOBA_BLOCK // PAGE_SIZE
N_PAGES = PAST_LEN // PAGE_SIZE
T_PAD = 8
QCOLS = 128
N_PART = 16

assert PAGES_PER_BLOCK == 2 and PAST_LEN % MOBA_BLOCK == 0 and N_PAST_BLOCKS >= MOBA_TOPK
assert N_HEADS * T_PAD <= QCOLS and N_PAST_BLOCKS < N_PART

VMEM_LIMIT = 56 * 1024 * 1024

_NT = (((1,), (1,)), ((), ()))
_TN = (((0,), (0,)), ((), ()))


def _params(sem):
    return pltpu.CompilerParams(dimension_semantics=sem, vmem_limit_bytes=VMEM_LIMIT)


def _rms_unit(x):
    return x * lax.rsqrt(jnp.mean(x * x, axis=-1, keepdims=True) + EPS)


def _gelu(x):
    c = math.sqrt(2.0 / math.pi)
    return x * (0.5 * (1.0 + jnp.tanh(c * (x + 0.044715 * (x * x * x)))))


def _rotate(seg, c, s1, s2):
    half = ROT_DIM // 2
    return (seg * c + pltpu.roll(seg, HEAD_DIM - half, axis=1) * s1
            + pltpu.roll(seg, half, axis=1) * s2)


def _top3_mask(gate, valid, axis):
    n = gate.shape[axis]
    idx = lax.broadcasted_iota(jnp.int32, gate.shape, axis)
    g = jnp.where(valid, gate, -jnp.inf)
    sel = jnp.zeros(gate.shape, F32)
    for _ in range(MOBA_TOPK):
        m = jnp.max(g, axis=axis, keepdims=True)
        first = jnp.min(jnp.where(g == m, idx, n), axis=axis, keepdims=True)
        pick = idx == first
        sel = jnp.where(pick, 1.0, sel)
        g = jnp.where(pick, -jnp.inf, g)
    return jnp.where(valid, sel, 0.0)


def _split_specs(width, col_map=None):
    col = col_map or (lambda j: 0)
    return [
        pl.BlockSpec((TM, width), lambda i, j: (jnp.minimum(i, N_PROMPT_TILES - 1), col(j))),
        pl.BlockSpec((TM, width), lambda i, j: (0, col(j))),
    ]


def _norm_matmul_kernel(*refs, split_x, n_rope_tiles, tn):
    refs = list(refs)
    xs_ref = None
    x_ref = refs.pop(0)
    if split_x:
        xs_ref = refs.pop(0)
    g_ref, w_ref = refs.pop(0), refs.pop(0)
    if n_rope_tiles:
        c_ref, s1_ref, s2_ref = refs.pop(0), refs.pop(0), refs.pop(0)
    o_ref, h_ref = refs
    i, j = pl.program_id(0), pl.program_id(1)

    def norm_from(ref):
        h_ref[...] = (_rms_unit(ref[...]) * g_ref[...]).astype(BF16)

    if split_x:
        pl.when((j == 0) & (i < N_PROMPT_TILES))(lambda: norm_from(x_ref))
        pl.when((j == 0) & (i >= N_PROMPT_TILES))(lambda: norm_from(xs_ref))
    else:
        pl.when(j == 0)(lambda: norm_from(x_ref))

    acc = jnp.dot(h_ref[...], w_ref[...], preferred_element_type=F32)

    if n_rope_tiles:
        @pl.when(j < n_rope_tiles)
        def _():
            c, s1, s2 = c_ref[...], s1_ref[...], s2_ref[...]
            for hd in range(tn // HEAD_DIM):
                cols = slice(hd * HEAD_DIM, (hd + 1) * HEAD_DIM)
                o_ref[:, cols] = _rotate(acc[:, cols], c, s1, s2)

        @pl.when(j >= n_rope_tiles)
        def _():
            o_ref[...] = acc
    else:
        o_ref[...] = acc


def _norm_matmul(x, g, w, rope=None, n_rope_cols=0, tn=512):
    split_x = isinstance(x, tuple)
    k, n = w.shape
    n_rope_tiles = n_rope_cols // tn
    if split_x:
        in_specs, args = _split_specs(k), list(x)
    else:
        in_specs, args = [pl.BlockSpec((TM, k), lambda i, j: (i, 0))], [x]
    in_specs += [pl.BlockSpec((1, k), lambda i, j: (0, 0)), pl.BlockSpec((k, tn), lambda i, j: (0, j))]
    args += [g.reshape(1, k), w]
    if n_rope_tiles:
        in_specs += [pl.BlockSpec((TM, HEAD_DIM), lambda i, j: (i, 0))] * 3
        args += list(rope)
    return pl.pallas_call(
        functools.partial(_norm_matmul_kernel, split_x=split_x, n_rope_tiles=n_rope_tiles, tn=tn),
        grid=(M_ALL // TM, n // tn),
        in_specs=in_specs,
        out_specs=pl.BlockSpec((TM, tn), lambda i, j: (i, j)),
        out_shape=jax.ShapeDtypeStruct((M_ALL, n), F32),
        scratch_shapes=[pltpu.VMEM((TM, k), BF16)],
        compiler_params=_params(("parallel", "arbitrary")),
        name="norm_matmul",
    )(*args)


KV_TN = 512
KV_HEADS_PER_TILE = KV_TN // HEAD_DIM
KV_K_TILES = MIX_WIDTH // KV_TN
ROW_TILES_PER_SEQ = SEQ // TM


def _kv_proj_kernel(x_ref, g_ref, w_ref, c_ref, s1_ref, s2_ref, kp_ref, vp_ref, kvs_ref, h_ref):
    i, j = pl.program_id(0), pl.program_id(1)

    @pl.when(j == 0)
    def _():
        h_ref[...] = (_rms_unit(x_ref[...]) * g_ref[...]).astype(BF16)

    acc = jnp.dot(h_ref[...], w_ref[...], preferred_element_type=F32)
    is_prompt = i < N_PROMPT_TILES
    is_key = j < KV_K_TILES

    def heads(rotary):
        for hd in range(KV_HEADS_PER_TILE):
            seg = acc[:, hd * HEAD_DIM:(hd + 1) * HEAD_DIM]
            yield hd, (_rotate(seg, c_ref[...], s1_ref[...], s2_ref[...]) if rotary else seg)

    @pl.when(is_prompt & is_key)
    def _():
        for hd, seg in heads(True):
            kp_ref[hd] = seg

    @pl.when(is_prompt & jnp.logical_not(is_key))
    def _():
        for hd, seg in heads(False):
            vp_ref[hd] = seg

    @pl.when(jnp.logical_not(is_prompt) & is_key)
    def _():
        for hd, seg in heads(True):
            kvs_ref[:, hd * HEAD_DIM:(hd + 1) * HEAD_DIM] = seg

    @pl.when(jnp.logical_not(is_prompt) & jnp.logical_not(is_key))
    def _():
        kvs_ref[...] = acc


def _kv_proj(x, g, w, rope):
    k = x.shape[1]
    last_k = KV_K_TILES - 1

    def prompt_map(col_of_j):
        def index(i, j):
            ic = jnp.minimum(i, N_PROMPT_TILES - 1)
            col = jnp.where(i < N_PROMPT_TILES, col_of_j(j), last_k)
            return (ic // ROW_TILES_PER_SEQ, col, ic % ROW_TILES_PER_SEQ, 0)
        return index

    head_block = (None, KV_HEADS_PER_TILE, TM, HEAD_DIM)
    return pl.pallas_call(
        _kv_proj_kernel,
        grid=(M_ALL // TM, 2 * MIX_WIDTH // KV_TN),
        in_specs=[
            pl.BlockSpec((TM, k), lambda i, j: (i, 0)),
            pl.BlockSpec((1, k), lambda i, j: (0, 0)),
            pl.BlockSpec((k, KV_TN), lambda i, j: (0, j)),
        ] + [pl.BlockSpec((TM, HEAD_DIM), lambda i, j: (i, 0))] * 3,
        out_specs=[
            pl.BlockSpec(head_block, prompt_map(lambda j: jnp.minimum(j, last_k))),
            pl.BlockSpec(head_block, prompt_map(lambda j: jnp.maximum(j - KV_K_TILES, 0))),
            pl.BlockSpec((TM, KV_TN), lambda i, j: (0, jnp.where(i < N_PROMPT_TILES, 0, j))),
        ],
        out_shape=[
            jax.ShapeDtypeStruct((BATCH, N_HEADS, SEQ, HEAD_DIM), F32),
            jax.ShapeDtypeStruct((BATCH, N_HEADS, SEQ, HEAD_DIM), F32),
            jax.ShapeDtypeStruct((M_SAMPLE, 2 * MIX_WIDTH), F32),
        ],
        scratch_shapes=[pltpu.VMEM((TM, k), BF16)],
        compiler_params=_params(("arbitrary", "arbitrary")),
        name="kv_proj",
    )(x, g.reshape(1, k), w, *rope)


def _mem_kv_kernel(x_ref, g_ref, w_ref, k_ref, v_ref):
    h = (_rms_unit(x_ref[...]) * g_ref[...]).astype(BF16)
    acc = jnp.dot(h, w_ref[...], preferred_element_type=F32)
    rows = acc.shape[0]
    for hd in range(N_MEM_HEADS):
        k_ref[pl.ds(hd, rows, stride=N_MEM_HEADS), :] = acc[:, hd * HEAD_DIM:(hd + 1) * HEAD_DIM]
        v_ref[pl.ds(hd, rows, stride=N_MEM_HEADS), :] = acc[:, MEM_WIDTH + hd * HEAD_DIM:
                                                             MEM_WIDTH + (hd + 1) * HEAD_DIM]


def _mem_kv(mem_rows, g_mem, w_mem_kv):
    depth = g_mem.shape[0]
    out_block = pl.BlockSpec((None, None, N_MEM * N_MEM_HEADS, HEAD_DIM), lambda l, b: (l, b, 0, 0))
    out_shape = jax.ShapeDtypeStruct((depth, BATCH, N_MEM * N_MEM_HEADS, HEAD_DIM), F32)
    return pl.pallas_call(
        _mem_kv_kernel,
        grid=(depth, BATCH),
        in_specs=[
            pl.BlockSpec((N_MEM, D_MODEL), lambda l, b: (b, 0)),
            pl.BlockSpec((None, 1, D_MODEL), lambda l, b: (l, 0, 0)),
            pl.BlockSpec((None, D_MODEL, 2 * MEM_WIDTH), lambda l, b: (l, 0, 0)),
        ],
        out_specs=[out_block, out_block],
        out_shape=[out_shape, out_shape],
        compiler_params=_params(("parallel", "parallel")),
        name="mem_kv",
    )(mem_rows, g_mem.reshape(depth, 1, D_MODEL), w_mem_kv)


def _gmlp_kernel(z_ref, gv_ref, wmix_ref, bias_ref, mix_ref, v_ref):
    u = _gelu(z_ref[:, :MIX_WIDTH])
    v = _rms_unit(_gelu(z_ref[:, MIX_WIDTH:2 * MIX_WIDTH])) * gv_ref[...]
    v_ref[...] = v
    vb = v.astype(BF16)
    bias = bias_ref[...]
    for g in range(N_GROUPS):
        lo, hi = g * HEAD_DIM, (g + 1) * HEAD_DIM
        mixed = jnp.dot(wmix_ref[g], vb[:, lo:hi], preferred_element_type=F32)
        mix_ref[:, lo:hi] = (u[:, lo:hi] * (mixed + bias[:, g:g + 1])).astype(BF16)


def _gmlp(z, g_v, wmix, bias):
    n_prompt_tiles = M_PROMPT // CHUNK
    return pl.pallas_call(
        _gmlp_kernel,
        grid=(M_ALL // CHUNK,),
        in_specs=[
            pl.BlockSpec((CHUNK, z.shape[1]), lambda i: (i, 0)),
            pl.BlockSpec((1, MIX_WIDTH), lambda i: (0, 0)),
            pl.BlockSpec((None, N_GROUPS, CHUNK, CHUNK), lambda i: (i // n_prompt_tiles, 0, 0, 0)),
            pl.BlockSpec((None, CHUNK, N_GROUPS), lambda i: (i // n_prompt_tiles, 0, 0)),
        ],
        out_specs=[
            pl.BlockSpec((CHUNK, MIX_WIDTH), lambda i: (i, 0)),
            pl.BlockSpec((CHUNK, MIX_WIDTH), lambda i: (jnp.maximum(i - n_prompt_tiles, 0), 0)),
        ],
        out_shape=[
            jax.ShapeDtypeStruct((M_ALL, MIX_WIDTH), BF16),
            jax.ShapeDtypeStruct((M_SAMPLE, MIX_WIDTH), F32),
        ],
        compiler_params=_params(("arbitrary",)),
        name="gmlp",
    )(z, g_v.reshape(1, MIX_WIDTH), wmix, bias)


def _mem_attend_one(q, k_ref, v_ref, o_ref, rows):
    for h in range(N_MEM_HEADS):
        cols = slice(h * HEAD_DIM, (h + 1) * HEAD_DIM)
        k = k_ref[pl.ds(h, N_MEM, stride=N_MEM_HEADS), :].astype(BF16)
        v = v_ref[pl.ds(h, N_MEM, stride=N_MEM_HEADS), :].astype(BF16)
        s = lax.dot_general(q[:, cols].astype(BF16), k, _NT, preferred_element_type=F32) * SCALE
        p = jnp.exp(s - jnp.max(s, axis=-1, keepdims=True))
        l = jnp.sum(p, axis=-1, keepdims=True)
        o = jnp.dot(p.astype(BF16), v, preferred_element_type=F32)
        o_ref[rows, cols] = (o / l).astype(o_ref.dtype)


def _mem_prompt_kernel(q_ref, k_ref, v_ref, o_ref):
    _mem_attend_one(q_ref[...], k_ref, v_ref, o_ref, slice(None))


def _mem_attend_prompt(z, q_col_block, mem_k, mem_v, layer, tq=512):
    nq = SEQ // tq
    kv_spec = pl.BlockSpec((None, None, N_MEM * N_MEM_HEADS, HEAD_DIM), lambda b, i: (layer, b, 0, 0))
    return pl.pallas_call(
        _mem_prompt_kernel,
        grid=(BATCH, nq),
        in_specs=[pl.BlockSpec((tq, MEM_WIDTH), lambda b, i: (b * nq + i, q_col_block)), kv_spec, kv_spec],
        out_specs=pl.BlockSpec((tq, MEM_WIDTH), lambda b, i: (b * nq + i, 0)),
        out_shape=jax.ShapeDtypeStruct((M_PROMPT, MEM_WIDTH), BF16),
        compiler_params=_params(("parallel", "parallel")),
        name="mem_attend_prompt",
    )(z, mem_k, mem_v)


def _mem_sample_kernel(q_ref, k_ref, v_ref, o_ref, *, n_seq):
    for s in range(n_seq):
        rows = slice(s * DEC_SEQ, (s + 1) * DEC_SEQ)
        _mem_attend_one(q_ref[rows, :], k_ref.at[s], v_ref.at[s], o_ref, rows)


def _mem_attend_sample(z, q_col_block, mem_k, mem_v, layer, n_seq=8):
    rows = n_seq * DEC_SEQ
    first = M_PROMPT // rows
    kv_spec = pl.BlockSpec((None, n_seq, N_MEM * N_MEM_HEADS, HEAD_DIM), lambda i: (layer, i, 0, 0))
    return pl.pallas_call(
        functools.partial(_mem_sample_kernel, n_seq=n_seq),
        grid=(DEC_BATCH // n_seq,),
        in_specs=[pl.BlockSpec((rows, MEM_WIDTH), lambda i: (first + i, q_col_block)), kv_spec, kv_spec],
        out_specs=pl.BlockSpec((rows, MEM_WIDTH), lambda i: (i, 0)),
        out_shape=jax.ShapeDtypeStruct((M_SAMPLE, MEM_WIDTH), F32),
        compiler_params=_params(("parallel",)),
        name="mem_attend_sample",
    )(z, mem_k, mem_v)


def _out_proj_kernel(*refs, split_x):
    if split_x:
        x_ref, xs_ref, mixp_ref, mixs_ref, mop_ref, mos_ref, w1_ref, w2_ref, o_ref = refs
    else:
        x_ref, mixp_ref, mixs_ref, mop_ref, mos_ref, w1_ref, w2_ref, o_ref = refs
        xs_ref = x_ref
    i = pl.program_id(0)

    def run(x_r, mix_r, mo_r):
        acc = jnp.dot(mix_r[...].astype(BF16), w1_ref[...], preferred_element_type=F32)
        acc += jnp.dot(mo_r[...].astype(BF16), w2_ref[...], preferred_element_type=F32)
        o_ref[...] = x_r[...] + acc

    pl.when(i < N_PROMPT_TILES)(lambda: run(x_ref, mixp_ref, mop_ref))
    pl.when(i >= N_PROMPT_TILES)(lambda: run(xs_ref, mixs_ref, mos_ref))


def _out_proj(x, mix, mo, w_out, tn=1024):
    split_x = isinstance(x, tuple)
    if split_x:
        in_specs, args = _split_specs(tn, lambda j: j), list(x)
    else:
        in_specs, args = [pl.BlockSpec((TM, tn), lambda i, j: (i, j))], [x]
    in_specs += _split_specs(MIX_WIDTH) + _split_specs(MEM_WIDTH)
    in_specs += [
        pl.BlockSpec((MIX_WIDTH, tn), lambda i, j: (0, j)),
        pl.BlockSpec((MEM_WIDTH, tn), lambda i, j: (MIX_WIDTH // MEM_WIDTH, j)),
    ]
    args += [mix[0], mix[1], mo[0], mo[1], w_out, w_out]
    return pl.pallas_call(
        functools.partial(_out_proj_kernel, split_x=split_x),
        grid=(M_ALL // TM, D_MODEL // tn),
        in_specs=in_specs,
        out_specs=pl.BlockSpec((TM, tn), lambda i, j: (i, j)),
        out_shape=jax.ShapeDtypeStruct((M_ALL, D_MODEL), F32),
        compiler_params=_params(("parallel", "parallel")),
        name="out_proj",
    )(*args)


def _mlp_kernel(*refs, final_norm):
    if final_norm:
        x_ref, g_ref, wu_ref, wd_ref, gf_ref, yp_ref, ys_ref, h_ref, acc_ref = refs
    else:
        x_ref, g_ref, wu_ref, wd_ref, o_ref, h_ref, acc_ref = refs
    i, f = pl.program_id(0), pl.program_id(1)
    last = f == pl.num_programs(1) - 1

    @pl.when(f == 0)
    def _():
        h_ref[...] = (_rms_unit(x_ref[...]) * g_ref[...]).astype(BF16)
        acc_ref[...] = jnp.zeros_like(acc_ref)

    a = jnp.maximum(jnp.dot(h_ref[...], wu_ref[...], preferred_element_type=F32), 0.0)
    acc_ref[...] += jnp.dot((a * a).astype(BF16), wd_ref[...], preferred_element_type=F32)

    if final_norm:
        def finish(ref):
            ref[...] = _rms_unit(x_ref[...] + acc_ref[...]) * gf_ref[...]

        pl.when(last & (i < N_PROMPT_TILES))(lambda: finish(yp_ref))
        pl.when(last & (i >= N_PROMPT_TILES))(lambda: finish(ys_ref))
    else:
        @pl.when(last)
        def _():
            o_ref[...] = x_ref[...] + acc_ref[...]


def _mlp(x, g, w_up, w_down, g_final=None, tf=1024):
    d = x.shape[1]
    dff = w_up.shape[1]
    final_norm = g_final is not None
    in_specs = [
        pl.BlockSpec((TM, d), lambda i, f: (i, 0)),
        pl.BlockSpec((1, d), lambda i, f: (0, 0)),
        pl.BlockSpec((d, tf), lambda i, f: (0, f)),
        pl.BlockSpec((tf, d), lambda i, f: (f, 0)),
    ]
    args = [x, g.reshape(1, d), w_up, w_down]
    if final_norm:
        in_specs.append(pl.BlockSpec((1, d), lambda i, f: (0, 0)))
        args.append(g_final.reshape(1, d))
        out_specs = [
            pl.BlockSpec((TM, d), lambda i, f: (jnp.minimum(i, N_PROMPT_TILES - 1), 0)),
            pl.BlockSpec((TM, d), lambda i, f: (0, 0)),
        ]
        out_shape = [jax.ShapeDtypeStruct((M_PROMPT, d), F32), jax.ShapeDtypeStruct((M_SAMPLE, d), F32)]
    else:
        out_specs = pl.BlockSpec((TM, d), lambda i, f: (i, 0))
        out_shape = jax.ShapeDtypeStruct((M_ALL, d), F32)
    return pl.pallas_call(
        functools.partial(_mlp_kernel, final_norm=final_norm),
        grid=(M_ALL // TM, dff // tf),
        in_specs=in_specs,
        out_specs=out_specs,
        out_shape=out_shape,
        scratch_shapes=[pltpu.VMEM((TM, d), BF16), pltpu.VMEM((TM, d), F32)],
        compiler_params=_params(("arbitrary", "arbitrary")),
        name="mlp",
    )(*args)


def _moba_prompt_kernel(q_ref, k_ref, v_ref, o_ref, kb_ref, vb_ref, kmean_ref, m_ref, l_ref, acc_ref):
    qi = pl.program_id(2)
    tq = MOBA_BLOCK

    @pl.when(qi == 0)
    def _():
        k = k_ref[...]
        kb_ref[...] = k.astype(BF16)
        vb_ref[...] = v_ref[...].astype(BF16)
        kmean_ref[...] = jnp.mean(k.reshape(N_BLOCKS, MOBA_BLOCK, HEAD_DIM), axis=1)

    q = q_ref[...]
    qb = q.astype(BF16)

    gate = lax.dot_general(kmean_ref[...], q, _NT, precision=lax.Precision.HIGHEST,
                           preferred_element_type=F32)
    blk = lax.broadcasted_iota(jnp.int32, gate.shape, 0)
    sel = _top3_mask(gate, blk < qi, axis=0)
    eye = (lax.broadcasted_iota(jnp.int32, (N_BLOCKS, HEAD_DIM), 0)
           == lax.broadcasted_iota(jnp.int32, (N_BLOCKS, HEAD_DIM), 1)).astype(BF16)
    sel_cols = lax.dot_general(sel.astype(BF16), eye, _TN, preferred_element_type=F32)

    off = pl.multiple_of(qi * MOBA_BLOCK, MOBA_BLOCK)
    s = lax.dot_general(qb, kb_ref[pl.ds(off, MOBA_BLOCK), :], _NT, preferred_element_type=F32) * SCALE
    causal = (lax.broadcasted_iota(jnp.int32, (tq, MOBA_BLOCK), 1)
              <= lax.broadcasted_iota(jnp.int32, (tq, MOBA_BLOCK), 0))
    s = jnp.where(causal, s, -jnp.inf)
    m0 = jnp.max(s, axis=-1, keepdims=True)
    p = jnp.exp(s - m0)
    m_ref[...] = m0
    l_ref[...] = jnp.sum(p, axis=-1, keepdims=True)
    acc_ref[...] = jnp.dot(p.astype(BF16), vb_ref[pl.ds(off, MOBA_BLOCK), :], preferred_element_type=F32)

    for n in range(N_BLOCKS - 1):
        @pl.when(n < qi)
        def _():
            rows = slice(n * MOBA_BLOCK, (n + 1) * MOBA_BLOCK)
            sn = lax.dot_general(qb, kb_ref[rows, :], _NT, preferred_element_type=F32) * SCALE
            sn = jnp.where(sel_cols[:, n:n + 1] > 0.5, sn, -jnp.inf)
            m_prev = m_ref[...]
            m_new = jnp.maximum(m_prev, jnp.max(sn, axis=-1, keepdims=True))
            alpha = jnp.exp(m_prev - m_new)
            pn = jnp.exp(sn - m_new)
            l_ref[...] = alpha * l_ref[...] + jnp.sum(pn, axis=-1, keepdims=True)
            acc_ref[...] = alpha * acc_ref[...] + jnp.dot(pn.astype(BF16), vb_ref[rows, :],
                                                          preferred_element_type=F32)
            m_ref[...] = m_new

    o_ref[...] = (acc_ref[...] / l_ref[...]).astype(BF16)


def _moba_prompt(zq, kp, vp):
    tq = MOBA_BLOCK
    kv_spec = pl.BlockSpec((None, None, SEQ, HEAD_DIM), lambda b, h, i: (b, h, 0, 0))
    return pl.pallas_call(
        _moba_prompt_kernel,
        grid=(BATCH, N_HEADS, N_BLOCKS),
        in_specs=[pl.BlockSpec((tq, HEAD_DIM), lambda b, h, i: (b * N_BLOCKS + i, h)), kv_spec, kv_spec],
        out_specs=pl.BlockSpec((tq, HEAD_DIM), lambda b, h, i: (b * N_BLOCKS + i, h)),
        out_shape=jax.ShapeDtypeStruct((M_PROMPT, MIX_WIDTH), BF16),
        scratch_shapes=[
            pltpu.VMEM((SEQ, HEAD_DIM), BF16),
            pltpu.VMEM((SEQ, HEAD_DIM), BF16),
            pltpu.VMEM((N_BLOCKS, HEAD_DIM), F32),
            pltpu.VMEM((tq, 1), F32),
            pltpu.VMEM((tq, 1), F32),
            pltpu.VMEM((tq, HEAD_DIM), F32),
        ],
        compiler_params=_params(("parallel", "parallel", "arbitrary")),
        name="moba_prompt",
    )(zq, kp, vp)


def _head_diag(x):
    return jnp.concatenate(
        [x[h * T_PAD:(h + 1) * T_PAD, h * HEAD_DIM:(h + 1) * HEAD_DIM] for h in range(N_HEADS)], axis=1)


def _heads_to_lanes(ref, fn):
    return jnp.concatenate([fn(ref[h]) for h in range(N_HEADS)], axis=1)


def _moba_sample_kernel(pt_ref, q_ref, kn_ref, vn_ref, hm_ref, k0_ref, k1_ref, v0_ref, v1_ref, o_ref,
                        qf_ref, qb_ref, kmean_ref, m_ref, l_ref, part_ref):
    del pt_ref
    n = pl.program_id(1)
    part_row = lax.broadcasted_iota(jnp.int32, (N_PART, QCOLS), 0)

    @pl.when(n == 0)
    def _():
        q8 = q_ref[...]
        qrep = jnp.broadcast_to(q8[None], (QCOLS // T_PAD, T_PAD, MIX_WIDTH)).reshape(QCOLS, MIX_WIDTH)
        qbd = qrep * hm_ref[...]
        qf_ref[...] = qbd
        qb_ref[...] = qbd.astype(BF16)
        kmean_ref[...] = jnp.zeros_like(kmean_ref)
        m_ref[...] = jnp.zeros_like(m_ref)
        l_ref[...] = jnp.zeros_like(l_ref)

    qb = qb_ref[...]

    def scores_t(k_bf):
        return lax.dot_general(k_bf, qb, _NT, preferred_element_type=F32) * SCALE

    to_bf = lambda x: x.astype(BF16)
    s0 = scores_t(_heads_to_lanes(k0_ref, to_bf))
    s1 = scores_t(_heads_to_lanes(k1_ref, to_bf))
    mn = jnp.maximum(jnp.max(s0, axis=0, keepdims=True), jnp.max(s1, axis=0, keepdims=True))
    p0 = jnp.exp(s0 - mn)
    p1 = jnp.exp(s1 - mn)
    ln = jnp.sum(p0, axis=0, keepdims=True) + jnp.sum(p1, axis=0, keepdims=True)
    pv = jnp.dot(p0.T.astype(BF16), _heads_to_lanes(v0_ref, to_bf), preferred_element_type=F32)
    pv += jnp.dot(p1.T.astype(BF16), _heads_to_lanes(v1_ref, to_bf), preferred_element_type=F32)
    part_ref[n] = _head_diag(pv)
    m_ref[...] = jnp.where(part_row == n, mn, m_ref[...])
    l_ref[...] = jnp.where(part_row == n, ln, l_ref[...])
    col_sum = lambda x: jnp.sum(x, axis=0, keepdims=True)
    ksum = _heads_to_lanes(k0_ref, col_sum) + _heads_to_lanes(k1_ref, col_sum)
    mean_row = lax.broadcasted_iota(jnp.int32, kmean_ref.shape, 0)
    kmean_ref[...] = jnp.where(mean_row == n, ksum * (1.0 / MOBA_BLOCK), kmean_ref[...])

    @pl.when(n == N_PAST_BLOCKS - 1)
    def _():
        tail = jnp.zeros((128 - T_PAD, MIX_WIDTH), F32)
        knb = jnp.concatenate([kn_ref[...], tail], axis=0).astype(BF16)
        vnb = jnp.concatenate([vn_ref[...], tail], axis=0).astype(BF16)
        sc = scores_t(knb)
        key_t = lax.broadcasted_iota(jnp.int32, sc.shape, 0)
        qry_t = lax.broadcasted_iota(jnp.int32, sc.shape, 1) % T_PAD
        sc = jnp.where(key_t <= jnp.minimum(qry_t, DEC_SEQ - 1), sc, -jnp.inf)
        mc = jnp.max(sc, axis=0, keepdims=True)
        pc = jnp.exp(sc - mc)
        lc = jnp.sum(pc, axis=0, keepdims=True)
        oc = _head_diag(jnp.dot(pc.T.astype(BF16), vnb, preferred_element_type=F32))
        m_all = jnp.where(part_row == N_PAST_BLOCKS, mc, m_ref[...])
        l_all = jnp.where(part_row == N_PAST_BLOCKS, lc, l_ref[...])

        gate = lax.dot_general(kmean_ref[...], qf_ref[...], _NT, precision=lax.Precision.HIGHEST,
                               preferred_element_type=F32)
        sel = _top3_mask(gate, part_row < N_PAST_BLOCKS, axis=0)
        sel = jnp.where(part_row == N_PAST_BLOCKS, 1.0, sel)
        m_tot = jnp.max(jnp.where(sel > 0.5, m_all, -jnp.inf), axis=0, keepdims=True)
        w = jnp.where(sel > 0.5, jnp.exp(m_all - m_tot), 0.0)
        w = w * (1.0 / jnp.sum(w * l_all, axis=0, keepdims=True))
        w_cols = jnp.concatenate([w, jnp.zeros((QCOLS - N_PART, QCOLS), F32)], axis=0).T
        for h in range(N_HEADS):
            rows = slice(h * T_PAD, (h + 1) * T_PAD)
            cols = slice(h * HEAD_DIM, (h + 1) * HEAD_DIM)
            acc = w_cols[rows, N_PAST_BLOCKS:N_PAST_BLOCKS + 1] * oc[:, cols]
            for b in range(N_PAST_BLOCKS):
                acc += w_cols[rows, b:b + 1] * part_ref[b, :, cols]
            o_ref[:, cols] = acc


def _moba_sample(q8, kn8, vn8, cache_k, cache_v, pt_flat):
    head_of_col = jnp.arange(MIX_WIDTH, dtype=jnp.int32) // HEAD_DIM
    head_of_row = jnp.arange(QCOLS, dtype=jnp.int32) // T_PAD
    head_mask = (head_of_row[:, None] == head_of_col[None, :]).astype(F32)

    def tok_spec():
        return pl.BlockSpec((None, T_PAD, MIX_WIDTH), lambda s, n, pt: (s, 0, 0))

    def page_spec(j):
        return pl.BlockSpec((None, N_HEADS, PAGE_SIZE, HEAD_DIM),
                            lambda s, n, pt: (pt[s * N_PAGES + n * PAGES_PER_BLOCK + j], 0, 0, 0))

    grid_spec = pltpu.PrefetchScalarGridSpec(
        num_scalar_prefetch=1,
        grid=(DEC_BATCH, N_PAST_BLOCKS),
        in_specs=[
            tok_spec(), tok_spec(), tok_spec(),
            pl.BlockSpec((QCOLS, MIX_WIDTH), lambda s, n, pt: (0, 0)),
            page_spec(0), page_spec(1), page_spec(0), page_spec(1),
        ],
        out_specs=pl.BlockSpec((None, T_PAD, MIX_WIDTH), lambda s, n, pt: (s, 0, 0)),
        scratch_shapes=[
            pltpu.VMEM((QCOLS, MIX_WIDTH), F32),
            pltpu.VMEM((QCOLS, MIX_WIDTH), BF16),
            pltpu.VMEM((N_PART, MIX_WIDTH), F32),
            pltpu.VMEM((N_PART, QCOLS), F32),
            pltpu.VMEM((N_PART, QCOLS), F32),
            pltpu.VMEM((N_PAST_BLOCKS, T_PAD, MIX_WIDTH), F32),
        ],
    )
    return pl.pallas_call(
        _moba_sample_kernel,
        grid_spec=grid_spec,
        out_shape=jax.ShapeDtypeStruct((DEC_BATCH, T_PAD, MIX_WIDTH), F32),
        compiler_params=_params(("parallel", "arbitrary")),
        name="moba_sample",
    )(pt_flat, q8, kn8, vn8, head_mask, cache_k, cache_k, cache_v, cache_v)


def _rope_tables():
    half = ROT_DIM // 2
    inv = ROPE_THETA ** (-(jnp.arange(half, dtype=F32) * 2.0) / ROT_DIM)
    pos = jnp.concatenate([
        jnp.tile(jnp.arange(SEQ, dtype=jnp.int32), BATCH),
        jnp.tile(PAST_LEN + jnp.arange(DEC_SEQ, dtype=jnp.int32), DEC_BATCH),
    ])
    ang = pos.astype(F32)[:, None] * inv[None, :]
    cos, sin = jnp.cos(ang), jnp.sin(ang)
    zeros = jnp.zeros_like(cos)
    pad = jnp.zeros((M_ALL, HEAD_DIM - ROT_DIM), F32)
    c = jnp.concatenate([cos, cos, pad + 1.0], axis=1)
    s1 = jnp.concatenate([-sin, zeros, pad], axis=1)
    s2 = jnp.concatenate([zeros, sin, pad], axis=1)
    return c, s1, s2


def _pad_tokens(rows):
    x = rows.reshape(DEC_BATCH, DEC_SEQ, rows.shape[-1])
    return jnp.pad(x, ((0, 0), (0, T_PAD - DEC_SEQ), (0, 0)))


def kernel(x_prompt, x_sample, cache_k, cache_v, cache_mem_k, cache_mem_v, page_table, mem_prompt,
           g_mix, w_in_a, w_in_b, g_v, w_s, b_s, w_out, g_mlp, w_up, w_down, g_mem, w_mem_kv,
           g_kv, w_kv, g_final):
    depth = g_mix.shape[0]
    assert depth == 2 and w_in_a.shape[0] == 1 and w_in_b.shape[0] == 1

    x0 = (x_prompt.reshape(M_PROMPT, D_MODEL), x_sample.reshape(M_SAMPLE, D_MODEL))
    rope = _rope_tables()

    mem_rows = N_MEM * N_MEM_HEADS
    mem_k_p, mem_v_p = _mem_kv(mem_prompt.reshape(BATCH * N_MEM, D_MODEL), g_mem, w_mem_kv.astype(BF16))
    mem_k_s = cache_mem_k.reshape(depth, DEC_BATCH, mem_rows, HEAD_DIM)
    mem_v_s = cache_mem_v.reshape(depth, DEC_BATCH, mem_rows, HEAD_DIM)

    z = _norm_matmul(x0, g_mix[0], w_in_a[0].astype(BF16))
    tril = jnp.tril(jnp.ones((CHUNK, CHUNK), bool))
    wmix_p = jnp.where(tril[None], w_s[0], 0.0)
    w_small = jnp.where(tril[None, :DEC_SEQ, :DEC_SEQ], w_s[0][:, :DEC_SEQ, :DEC_SEQ], 0.0)
    n_rep = CHUNK // DEC_SEQ
    wmix_s = jnp.einsum('ab,gts->gatbs', jnp.eye(n_rep, dtype=F32), w_small).reshape(N_GROUPS, CHUNK, CHUNK)
    wmix = jnp.stack([wmix_p, wmix_s]).astype(BF16)
    bias = jnp.stack([b_s[0].T, jnp.tile(b_s[0][:, :DEC_SEQ].T, (n_rep, 1))])
    mix, v_rows = _gmlp(z, g_v[0], wmix, bias)
    q_blk = 2 * MIX_WIDTH // MEM_WIDTH
    mo = (_mem_attend_prompt(z, q_blk, mem_k_p, mem_v_p, 0),
          _mem_attend_sample(z, q_blk, mem_k_s, mem_v_s, 0))
    x = _out_proj(x0, (mix, mix[M_PROMPT:]), mo, w_out[0].astype(BF16))
    x = _mlp(x, g_mlp[0], w_up[0].astype(BF16), w_down[0].astype(BF16))

    kp, vp, kvs = _kv_proj(x, g_kv, w_kv.astype(BF16), rope)
    zq = _norm_matmul(x, g_mix[1], w_in_b[0].astype(BF16), rope=rope, n_rope_cols=MIX_WIDTH)
    moba_p = _moba_prompt(zq, kp, vp)
    moba_s = _moba_sample(
        _pad_tokens(zq[M_PROMPT:, :MIX_WIDTH]),
        _pad_tokens(kvs[:, :MIX_WIDTH]),
        _pad_tokens(kvs[:, MIX_WIDTH:]),
        jnp.transpose(cache_k, (0, 2, 1, 3)),
        jnp.transpose(cache_v, (0, 2, 1, 3)),
        page_table.reshape(-1),
    )
    moba_s = moba_s[:, :DEC_SEQ].reshape(M_SAMPLE, MIX_WIDTH)
    q_blk = MIX_WIDTH // MEM_WIDTH
    mo = (_mem_attend_prompt(zq, q_blk, mem_k_p, mem_v_p, 1),
          _mem_attend_sample(zq, q_blk, mem_k_s, mem_v_s, 1))
    x = _out_proj(x, (moba_p, moba_s), mo, w_out[1].astype(BF16))
    y_p, y_s = _mlp(x, g_mlp[1], w_up[1].astype(BF16), w_down[1].astype(BF16), g_final=g_final)

    y_prompt = y_p.reshape(BATCH, SEQ, D_MODEL)
    y_sample = y_s.reshape(DEC_BATCH, DEC_SEQ, D_MODEL)
    k_prompt = jnp.transpose(kp, (0, 2, 1, 3))
    v_prompt = jnp.transpose(vp, (0, 2, 1, 3))
    k_sample = kvs[:, :MIX_WIDTH].reshape(DEC_BATCH, DEC_SEQ, N_HEADS, HEAD_DIM)
    v_sample = kvs[:, MIX_WIDTH:].reshape(DEC_BATCH, DEC_SEQ, N_HEADS, HEAD_DIM)
    mem_shape = (depth, BATCH, N_MEM, N_MEM_HEADS, HEAD_DIM)
    gmlp_v_sample = v_rows.reshape(1, DEC_BATCH, DEC_SEQ, MIX_WIDTH)
    return (y_prompt, y_sample, k_prompt, v_prompt, k_sample, v_sample,
            mem_k_p.reshape(mem_shape), mem_v_p.reshape(mem_shape), gmlp_v_sample)
```

```python
import functools
import math

import jax
import jax.numpy as jnp
from jax import lax
from jax.experimental import pallas as pl
from jax.experimental.pallas import tpu as pltpu

F32 = jnp.float32
BF16 = jnp.bfloat16

D_MODEL = 2048
BATCH = 4
SEQ = 2048
DEC_BATCH = 128
DEC_SEQ = 4
PAST_LEN = 2048
PAGE_SIZE = 128
HEAD_DIM = 128
N_MEM_HEADS = 4
MEM_WIDTH = N_MEM_HEADS * HEAD_DIM
MIX_WIDTH = D_MODEL - MEM_WIDTH
N_HEADS = MIX_WIDTH // HEAD_DIM
N_GROUPS = MIX_WIDTH // HEAD_DIM
CHUNK = 128
D_FF = 4 * D_MODEL
N_MEM = 256
MOBA_BLOCK = 256
MOBA_TOPK = 3
ROPE_THETA = 500000.0
ROT_DIM = HEAD_DIM // 4
EPS = 1e-6
SCALE = HEAD_DIM ** -0.5

M_PROMPT = BATCH * SEQ
M_SAMPLE = DEC_BATCH * DEC_SEQ
M_ALL = M_PROMPT + M_SAMPLE
TM = M_SAMPLE
N_PROMPT_TILES = M_PROMPT // TM
N_BLOCKS = SEQ // MOBA_BLOCK
N_PAST_BLOCKS = PAST_LEN // MOBA_BLOCK
PAGES_PER_BLOCK = MOBA_BLOCK // PAGE_SIZE
N_PAGES = PAST_LEN // PAGE_SIZE
T_PAD = 8
QCOLS = 128
N_PART = 16

assert PAGES_PER_BLOCK == 2 and PAST_LEN % MOBA_BLOCK == 0 and N_PAST_BLOCKS >= MOBA_TOPK
assert N_HEADS * T_PAD <= QCOLS and N_PAST_BLOCKS < N_PART

VMEM_LIMIT = 56 * 1024 * 1024

EXP2_SCALE = SCALE * math.log2(math.e)
MASKED = -1e30

_NT = (((1,), (1,)), ((), ()))
_TN = (((0,), (0,)), ((), ()))


def _params(sem):
    return pltpu.CompilerParams(dimension_semantics=sem, vmem_limit_bytes=VMEM_LIMIT)


def _rms_unit(x):
    return x * lax.rsqrt(jnp.mean(x * x, axis=-1, keepdims=True) + EPS)


def _gelu(x):
    c = math.sqrt(2.0 / math.pi)
    return x * (0.5 * (1.0 + jnp.tanh(c * (x + 0.044715 * (x * x * x)))))


def _rotate(seg, c, s1, s2):
    half = ROT_DIM // 2
    return (seg * c + pltpu.roll(seg, HEAD_DIM - half, axis=1) * s1
            + pltpu.roll(seg, half, axis=1) * s2)


def _dot_nt_3pass(a, b):
    def split(x):
        hi = x.astype(BF16)
        return hi, (x - hi.astype(F32)).astype(BF16)

    def nt(x, y):
        return lax.dot_general(x, y, _NT, preferred_element_type=F32)

    (ah, al), (bh, bl) = split(a), split(b)
    return nt(ah, bh) + (nt(ah, bl) + nt(al, bh))


def _top3_mask(gate, valid, axis):
    n = gate.shape[axis]
    idx = lax.broadcasted_iota(jnp.int32, gate.shape, axis)
    g = jnp.where(valid, gate, -jnp.inf)
    sel = jnp.zeros(gate.shape, F32)
    for _ in range(MOBA_TOPK):
        m = jnp.max(g, axis=axis, keepdims=True)
        first = jnp.min(jnp.where(g == m, idx, n), axis=axis, keepdims=True)
        pick = idx == first
        sel = jnp.where(pick, 1.0, sel)
        g = jnp.where(pick, -jnp.inf, g)
    return jnp.where(valid, sel, 0.0)


def _split_specs(width, col_map=None):
    col = col_map or (lambda j: 0)
    return [
        pl.BlockSpec((TM, width), lambda i, j: (jnp.minimum(i, N_PROMPT_TILES - 1), col(j))),
        pl.BlockSpec((TM, width), lambda i, j: (0, col(j))),
    ]


def _norm_matmul_kernel(*refs, split_x, n_rope_heads):
    refs = list(refs)
    xs_ref = None
    x_ref = refs.pop(0)
    if split_x:
        xs_ref = refs.pop(0)
    g_ref, w_ref = refs.pop(0), refs.pop(0)
    if n_rope_heads:
        c_ref, s1_ref, s2_ref = refs.pop(0), refs.pop(0), refs.pop(0)
    o_ref, h_ref = refs
    i, j = pl.program_id(0), pl.program_id(1)

    def norm_from(ref):
        h_ref[...] = (_rms_unit(ref[...]) * g_ref[...]).astype(BF16)

    if split_x:
        pl.when((j == 0) & (i < N_PROMPT_TILES))(lambda: norm_from(x_ref))
        pl.when((j == 0) & (i >= N_PROMPT_TILES))(lambda: norm_from(xs_ref))
    else:
        pl.when(j == 0)(lambda: norm_from(x_ref))

    acc = jnp.dot(h_ref[...], w_ref[...], preferred_element_type=F32)

    if n_rope_heads:
        c, s1, s2 = c_ref[...], s1_ref[...], s2_ref[...]
        for hd in range(n_rope_heads):
            cols = slice(hd * HEAD_DIM, (hd + 1) * HEAD_DIM)
            o_ref[:, cols] = _rotate(acc[:, cols], c, s1, s2)
        o_ref[:, n_rope_heads * HEAD_DIM:] = acc[:, n_rope_heads * HEAD_DIM:]
    else:
        o_ref[...] = acc


def _norm_matmul(x, g, w, tn, rope=None, n_rope_cols=0):
    split_x = isinstance(x, tuple)
    k, n = w.shape
    assert n % tn == 0 and (n_rope_cols == 0 or tn == n)
    if split_x:
        in_specs, args = _split_specs(k), list(x)
    else:
        in_specs, args = [pl.BlockSpec((TM, k), lambda i, j: (i, 0))], [x]
    in_specs += [pl.BlockSpec((1, k), lambda i, j: (0, 0)), pl.BlockSpec((k, tn), lambda i, j: (0, j))]
    args += [g.reshape(1, k), w]
    if n_rope_cols:
        in_specs += [pl.BlockSpec((TM, HEAD_DIM), lambda i, j: (i, 0))] * 3
        args += list(rope)
    return pl.pallas_call(
        functools.partial(_norm_matmul_kernel, split_x=split_x, n_rope_heads=n_rope_cols // HEAD_DIM),
        grid=(M_ALL // TM, n // tn),
        in_specs=in_specs,
        out_specs=pl.BlockSpec((TM, tn), lambda i, j: (i, j)),
        out_shape=jax.ShapeDtypeStruct((M_ALL, n), F32),
        scratch_shapes=[pltpu.VMEM((TM, k), BF16)],
        compiler_params=_params(("parallel", "arbitrary")),
        name="norm_matmul",
    )(*args)


ROW_TILES_PER_SEQ = SEQ // TM


def _kv_proj_kernel(x_ref, g_ref, w_ref, c_ref, s1_ref, s2_ref, kp_ref, vp_ref, kvs_ref, h_ref):
    i, j = pl.program_id(0), pl.program_id(1)

    @pl.when(j == 0)
    def _():
        h_ref[...] = (_rms_unit(x_ref[...]) * g_ref[...]).astype(BF16)

    acc = jnp.dot(h_ref[...], w_ref[...], preferred_element_type=F32)
    is_prompt = i < N_PROMPT_TILES

    def heads(rotary):
        for hd in range(N_HEADS):
            seg = acc[:, hd * HEAD_DIM:(hd + 1) * HEAD_DIM]
            yield hd, (_rotate(seg, c_ref[...], s1_ref[...], s2_ref[...]) if rotary else seg)

    @pl.when(is_prompt & (j == 0))
    def _():
        for hd, seg in heads(True):
            kp_ref[hd] = seg

    @pl.when(is_prompt & (j == 1))
    def _():
        for hd, seg in heads(False):
            vp_ref[hd] = seg

    @pl.when(jnp.logical_not(is_prompt) & (j == 0))
    def _():
        for hd, seg in heads(True):
            kvs_ref[:, hd * HEAD_DIM:(hd + 1) * HEAD_DIM] = seg

    @pl.when(jnp.logical_not(is_prompt) & (j == 1))
    def _():
        kvs_ref[...] = acc


def _kv_proj(x, g, w, rope):
    k = x.shape[1]

    def prompt_map(i, j):
        ic = jnp.minimum(i, N_PROMPT_TILES - 1)
        return (ic // ROW_TILES_PER_SEQ, 0, ic % ROW_TILES_PER_SEQ, 0)

    head_block = (None, N_HEADS, TM, HEAD_DIM)
    return pl.pallas_call(
        _kv_proj_kernel,
        grid=(M_ALL // TM, 2),
        in_specs=[
            pl.BlockSpec((TM, k), lambda i, j: (i, 0)),
            pl.BlockSpec((1, k), lambda i, j: (0, 0)),
            pl.BlockSpec((k, MIX_WIDTH), lambda i, j: (0, j)),
        ] + [pl.BlockSpec((TM, HEAD_DIM), lambda i, j: (i, 0))] * 3,
        out_specs=[
            pl.BlockSpec(head_block, prompt_map),
            pl.BlockSpec(head_block, prompt_map),
            pl.BlockSpec((TM, MIX_WIDTH), lambda i, j: (0, jnp.where(i < N_PROMPT_TILES, 0, j))),
        ],
        out_shape=[
            jax.ShapeDtypeStruct((BATCH, N_HEADS, SEQ, HEAD_DIM), F32),
            jax.ShapeDtypeStruct((BATCH, N_HEADS, SEQ, HEAD_DIM), F32),
            jax.ShapeDtypeStruct((M_SAMPLE, 2 * MIX_WIDTH), F32),
        ],
        scratch_shapes=[pltpu.VMEM((TM, k), BF16)],
        compiler_params=_params(("arbitrary", "arbitrary")),
        name="kv_proj",
    )(x, g.reshape(1, k), w, *rope)


def _mem_kv_kernel(x_ref, g_ref, w_ref, k_ref, v_ref):
    h = (_rms_unit(x_ref[...]) * g_ref[...]).astype(BF16)
    acc = jnp.dot(h, w_ref[...], preferred_element_type=F32)
    rows = acc.shape[0]
    for hd in range(N_MEM_HEADS):
        k_ref[pl.ds(hd, rows, stride=N_MEM_HEADS), :] = acc[:, hd * HEAD_DIM:(hd + 1) * HEAD_DIM]
        v_ref[pl.ds(hd, rows, stride=N_MEM_HEADS), :] = acc[:, MEM_WIDTH + hd * HEAD_DIM:
                                                             MEM_WIDTH + (hd + 1) * HEAD_DIM]


def _mem_kv(mem_rows, g_mem, w_mem_kv):
    depth = g_mem.shape[0]
    out_block = pl.BlockSpec((None, None, N_MEM * N_MEM_HEADS, HEAD_DIM), lambda l, b: (l, b, 0, 0))
    out_shape = jax.ShapeDtypeStruct((depth, BATCH, N_MEM * N_MEM_HEADS, HEAD_DIM), F32)
    return pl.pallas_call(
        _mem_kv_kernel,
        grid=(depth, BATCH),
        in_specs=[
            pl.BlockSpec((N_MEM, D_MODEL), lambda l, b: (b, 0)),
            pl.BlockSpec((None, 1, D_MODEL), lambda l, b: (l, 0, 0)),
            pl.BlockSpec((None, D_MODEL, 2 * MEM_WIDTH), lambda l, b: (l, 0, 0)),
        ],
        out_specs=[out_block, out_block],
        out_shape=[out_shape, out_shape],
        compiler_params=_params(("parallel", "parallel")),
        name="mem_kv",
    )(mem_rows, g_mem.reshape(depth, 1, D_MODEL), w_mem_kv)


def _gmlp_kernel(z_ref, gv_ref, wmix_ref, bias_ref, mix_ref, v_ref):
    u = _gelu(z_ref[:, :MIX_WIDTH])
    v = _rms_unit(_gelu(z_ref[:, MIX_WIDTH:2 * MIX_WIDTH])) * gv_ref[...]
    v_ref[...] = v
    vb = v.astype(BF16)
    bias = bias_ref[...]
    for g in range(N_GROUPS):
        lo, hi = g * HEAD_DIM, (g + 1) * HEAD_DIM
        mixed = jnp.dot(wmix_ref[g], vb[:, lo:hi], preferred_element_type=F32)
        mix_ref[:, lo:hi] = (u[:, lo:hi] * (mixed + bias[:, g:g + 1])).astype(BF16)


def _gmlp(z, g_v, wmix, bias):
    n_prompt_tiles = M_PROMPT // CHUNK
    return pl.pallas_call(
        _gmlp_kernel,
        grid=(M_ALL // CHUNK,),
        in_specs=[
            pl.BlockSpec((CHUNK, z.shape[1]), lambda i: (i, 0)),
            pl.BlockSpec((1, MIX_WIDTH), lambda i: (0, 0)),
            pl.BlockSpec((None, N_GROUPS, CHUNK, CHUNK), lambda i: (i // n_prompt_tiles, 0, 0, 0)),
            pl.BlockSpec((None, CHUNK, N_GROUPS), lambda i: (i // n_prompt_tiles, 0, 0)),
        ],
        out_specs=[
            pl.BlockSpec((CHUNK, MIX_WIDTH), lambda i: (i, 0)),
            pl.BlockSpec((CHUNK, MIX_WIDTH), lambda i: (jnp.maximum(i - n_prompt_tiles, 0), 0)),
        ],
        out_shape=[
            jax.ShapeDtypeStruct((M_ALL, MIX_WIDTH), BF16),
            jax.ShapeDtypeStruct((M_SAMPLE, MIX_WIDTH), F32),
        ],
        compiler_params=_params(("arbitrary",)),
        name="gmlp",
    )(z, g_v.reshape(1, MIX_WIDTH), wmix, bias)


def _mem_attend_one(q, k_ref, v_ref, o_ref, rows):
    for h in range(N_MEM_HEADS):
        cols = slice(h * HEAD_DIM, (h + 1) * HEAD_DIM)
        k = k_ref[pl.ds(h, N_MEM, stride=N_MEM_HEADS), :].astype(BF16)
        v = v_ref[pl.ds(h, N_MEM, stride=N_MEM_HEADS), :].astype(BF16)
        s = lax.dot_general(q[:, cols].astype(BF16), k, _NT, preferred_element_type=F32) * SCALE
        p = jnp.exp(s - jnp.max(s, axis=-1, keepdims=True))
        l = jnp.sum(p, axis=-1, keepdims=True)
        o = jnp.dot(p.astype(BF16), v, preferred_element_type=F32)
        o_ref[rows, cols] = (o / l).astype(o_ref.dtype)


def _mem_prompt_kernel(q_ref, k_ref, v_ref, o_ref):
    _mem_attend_one(q_ref[...], k_ref, v_ref, o_ref, slice(None))


def _mem_attend_prompt(z, q_col_block, mem_k, mem_v, layer, tq=512):
    nq = SEQ // tq
    kv_spec = pl.BlockSpec((None, None, N_MEM * N_MEM_HEADS, HEAD_DIM), lambda b, i: (layer, b, 0, 0))
    return pl.pallas_call(
        _mem_prompt_kernel,
        grid=(BATCH, nq),
        in_specs=[pl.BlockSpec((tq, MEM_WIDTH), lambda b, i: (b * nq + i, q_col_block)), kv_spec, kv_spec],
        out_specs=pl.BlockSpec((tq, MEM_WIDTH), lambda b, i: (b * nq + i, 0)),
        out_shape=jax.ShapeDtypeStruct((M_PROMPT, MEM_WIDTH), BF16),
        compiler_params=_params(("parallel", "parallel")),
        name="mem_attend_prompt",
    )(z, mem_k, mem_v)


def _mem_sample_kernel(q_ref, k_ref, v_ref, o_ref, *, n_seq):
    n_rows, n_cols = N_MEM_HEADS * T_PAD, N_MEM * N_MEM_HEADS
    same_head = (lax.broadcasted_iota(jnp.int32, (n_rows, n_cols), 1) % N_MEM_HEADS
                 == lax.broadcasted_iota(jnp.int32, (n_rows, n_cols), 0) // T_PAD)
    scores = []
    for s in range(n_seq):
        q8 = q_ref[s]
        q_all = jnp.concatenate([q8[:, h * HEAD_DIM:(h + 1) * HEAD_DIM] for h in range(N_MEM_HEADS)], axis=0)
        sc = lax.dot_general(q_all.astype(BF16), k_ref[s].astype(BF16), _NT,
                             preferred_element_type=F32) * SCALE
        scores.append(jnp.where(same_head, sc, -jnp.inf))
    probs = []
    for sc in scores:
        p = jnp.exp(sc - jnp.max(sc, axis=-1, keepdims=True))
        probs.append((p.astype(BF16), jnp.sum(p, axis=-1, keepdims=True)))
    for s, (p, l) in enumerate(probs):
        o = jnp.dot(p, v_ref[s].astype(BF16), preferred_element_type=F32) / l
        for h in range(N_MEM_HEADS):
            o_ref[s, :, h * HEAD_DIM:(h + 1) * HEAD_DIM] = o[h * T_PAD:(h + 1) * T_PAD]


def _mem_attend_sample(q8, mem_k, mem_v, layer, n_seq=8):
    kv_spec = pl.BlockSpec((None, n_seq, N_MEM * N_MEM_HEADS, HEAD_DIM), lambda i: (layer, i, 0, 0))
    tok_spec = pl.BlockSpec((n_seq, T_PAD, MEM_WIDTH), lambda i: (i, 0, 0))
    return pl.pallas_call(
        functools.partial(_mem_sample_kernel, n_seq=n_seq),
        grid=(DEC_BATCH // n_seq,),
        in_specs=[tok_spec, kv_spec, kv_spec],
        out_specs=tok_spec,
        out_shape=jax.ShapeDtypeStruct((DEC_BATCH, T_PAD, MEM_WIDTH), F32),
        compiler_params=_params(("parallel",)),
        name="mem_attend_sample",
    )(q8, mem_k, mem_v)


def _out_proj_kernel(*refs, split_x):
    if split_x:
        x_ref, xs_ref, mixp_ref, mixs_ref, mop_ref, mos_ref, w1_ref, w2_ref, o_ref = refs
    else:
        x_ref, mixp_ref, mixs_ref, mop_ref, mos_ref, w1_ref, w2_ref, o_ref = refs
        xs_ref = x_ref
    i = pl.program_id(0)

    def run(x_r, mix_r, mo_r):
        acc = jnp.dot(mix_r[...].astype(BF16), w1_ref[...], preferred_element_type=F32)
        acc += jnp.dot(mo_r[...].astype(BF16), w2_ref[...], preferred_element_type=F32)
        o_ref[...] = x_r[...] + acc

    pl.when(i < N_PROMPT_TILES)(lambda: run(x_ref, mixp_ref, mop_ref))
    pl.when(i >= N_PROMPT_TILES)(lambda: run(xs_ref, mixs_ref, mos_ref))


def _out_proj(x, mix, mo, w_out, layer, tn=D_MODEL):
    split_x = isinstance(x, tuple)
    if split_x:
        in_specs, args = _split_specs(tn, lambda j: j), list(x)
    else:
        in_specs, args = [pl.BlockSpec((TM, tn), lambda i, j: (i, j))], [x]
    in_specs += _split_specs(MIX_WIDTH) + _split_specs(MEM_WIDTH)
    in_specs += [
        pl.BlockSpec((None, MIX_WIDTH, tn), lambda i, j: (layer, 0, j)),
        pl.BlockSpec((None, MEM_WIDTH, tn), lambda i, j: (layer, MIX_WIDTH // MEM_WIDTH, j)),
    ]
    args += [mix[0], mix[1], mo[0], mo[1], w_out, w_out]
    return pl.pallas_call(
        functools.partial(_out_proj_kernel, split_x=split_x),
        grid=(M_ALL // TM, D_MODEL // tn),
        in_specs=in_specs,
        out_specs=pl.BlockSpec((TM, tn), lambda i, j: (i, j)),
        out_shape=jax.ShapeDtypeStruct((M_ALL, D_MODEL), F32),
        compiler_params=_params(("parallel", "parallel")),
        name="out_proj",
    )(*args)


def _mlp_kernel(*refs, final_norm):
    if final_norm:
        x_ref, g_ref, wu_ref, wd_ref, gf_ref, yp_ref, ys_ref, h_ref, acc_ref = refs
    else:
        x_ref, g_ref, wu_ref, wd_ref, o_ref, h_ref, acc_ref = refs
    i, f = pl.program_id(0), pl.program_id(1)
    last = f == pl.num_programs(1) - 1

    @pl.when(f == 0)
    def _():
        h_ref[...] = (_rms_unit(x_ref[...]) * g_ref[...]).astype(BF16)
        acc_ref[...] = jnp.zeros_like(acc_ref)

    a = jnp.maximum(jnp.dot(h_ref[...], wu_ref[...], preferred_element_type=F32), 0.0)
    acc_ref[...] += jnp.dot((a * a).astype(BF16), wd_ref[...], preferred_element_type=F32)

    if final_norm:
        def finish(ref):
            ref[...] = _rms_unit(x_ref[...] + acc_ref[...]) * gf_ref[...]

        pl.when(last & (i < N_PROMPT_TILES))(lambda: finish(yp_ref))
        pl.when(last & (i >= N_PROMPT_TILES))(lambda: finish(ys_ref))
    else:
        @pl.when(last)
        def _():
            o_ref[...] = x_ref[...] + acc_ref[...]


def _mlp(x, g, w_up, w_down, layer, g_final=None, tf=1024):
    d = x.shape[1]
    dff = w_up.shape[2]
    final_norm = g_final is not None
    in_specs = [
        pl.BlockSpec((TM, d), lambda i, f: (i, 0)),
        pl.BlockSpec((1, d), lambda i, f: (0, 0)),
        pl.BlockSpec((None, d, tf), lambda i, f: (layer, 0, f)),
        pl.BlockSpec((None, tf, d), lambda i, f: (layer, f, 0)),
    ]
    args = [x, g.reshape(1, d), w_up, w_down]
    if final_norm:
        in_specs.append(pl.BlockSpec((1, d), lambda i, f: (0, 0)))
        args.append(g_final.reshape(1, d))
        out_specs = [
            pl.BlockSpec((TM, d), lambda i, f: (jnp.minimum(i, N_PROMPT_TILES - 1), 0)),
            pl.BlockSpec((TM, d), lambda i, f: (0, 0)),
        ]
        out_shape = [jax.ShapeDtypeStruct((M_PROMPT, d), F32), jax.ShapeDtypeStruct((M_SAMPLE, d), F32)]
    else:
        out_specs = pl.BlockSpec((TM, d), lambda i, f: (i, 0))
        out_shape = jax.ShapeDtypeStruct((M_ALL, d), F32)
    return pl.pallas_call(
        functools.partial(_mlp_kernel, final_norm=final_norm),
        grid=(M_ALL // TM, dff // tf),
        in_specs=in_specs,
        out_specs=out_specs,
        out_shape=out_shape,
        scratch_shapes=[pltpu.VMEM((TM, d), BF16), pltpu.VMEM((TM, d), F32)],
        compiler_params=_params(("arbitrary", "arbitrary")),
        name="mlp",
    )(*args)


def _moba_prompt_kernel(q_ref, k_ref, v_ref, o_ref, qa_ref, ka_ref, va_ref):
    lane = lax.broadcasted_iota(jnp.int32, (SEQ, HEAD_DIM), 1)
    row_blk = lax.broadcasted_iota(jnp.int32, (SEQ, HEAD_DIM), 0) // MOBA_BLOCK
    k = k_ref[...]
    q = q_ref[...]

    ka_ref[:, :HEAD_DIM] = k.astype(BF16)
    ka_ref[:, HEAD_DIM:] = (lane == row_blk).astype(BF16)
    va_ref[:, :HEAD_DIM] = v_ref[...].astype(BF16)
    va_ref[:, HEAD_DIM:] = (lane == 0).astype(BF16)

    kmean = jnp.mean(k.reshape(N_BLOCKS, MOBA_BLOCK, HEAD_DIM), axis=1)
    gate = _dot_nt_3pass(kmean, q)
    blk = lax.broadcasted_iota(jnp.int32, gate.shape, 0)
    q_blk = lax.broadcasted_iota(jnp.int32, gate.shape, 1) // MOBA_BLOCK
    sel = _top3_mask(gate, blk < q_blk, axis=0)
    eye = (lax.broadcasted_iota(jnp.int32, (N_BLOCKS, HEAD_DIM), 0)
           == lax.broadcasted_iota(jnp.int32, (N_BLOCKS, HEAD_DIM), 1)).astype(BF16)
    sel_cols = lax.dot_general(sel.astype(BF16), eye, _TN, preferred_element_type=F32)
    qa_ref[:, :HEAD_DIM] = q.astype(BF16)
    qa_ref[:, HEAD_DIM:] = jnp.where(sel_cols > 0.5, 0.0, MASKED).astype(BF16)

    causal = (lax.broadcasted_iota(jnp.int32, (MOBA_BLOCK, MOBA_BLOCK), 1)
              <= lax.broadcasted_iota(jnp.int32, (MOBA_BLOCK, MOBA_BLOCK), 0))
    for qi in range(N_BLOCKS):
        rows = slice(qi * MOBA_BLOCK, (qi + 1) * MOBA_BLOCK)
        past = slice(0, qi * MOBA_BLOCK)
        s_own = lax.dot_general(qa_ref[rows, :HEAD_DIM], ka_ref[rows, :HEAD_DIM], _NT,
                                preferred_element_type=F32) * EXP2_SCALE
        s_own = jnp.where(causal, s_own, MASKED)
        m = jnp.max(s_own, axis=-1, keepdims=True)
        if qi:
            s_past = lax.dot_general(qa_ref[rows, :], ka_ref[past, :], _NT,
                                     preferred_element_type=F32) * EXP2_SCALE
            m = jnp.maximum(m, jnp.max(s_past, axis=-1, keepdims=True))
        pv = jnp.dot(jnp.exp2(s_own - m).astype(BF16), va_ref[rows, :], preferred_element_type=F32)
        if qi:
            pv += jnp.dot(jnp.exp2(s_past - m).astype(BF16), va_ref[past, :], preferred_element_type=F32)
        o_ref[rows, :] = (pv[:, :HEAD_DIM] / pv[:, HEAD_DIM:HEAD_DIM + 1]).astype(BF16)


def _moba_prompt(zq, kp, vp):
    kv_spec = pl.BlockSpec((None, None, SEQ, HEAD_DIM), lambda b, h: (b, h, 0, 0))
    return pl.pallas_call(
        _moba_prompt_kernel,
        grid=(BATCH, N_HEADS),
        in_specs=[pl.BlockSpec((SEQ, HEAD_DIM), lambda b, h: (b, h)), kv_spec, kv_spec],
        out_specs=pl.BlockSpec((SEQ, HEAD_DIM), lambda b, h: (b, h)),
        out_shape=jax.ShapeDtypeStruct((M_PROMPT, MIX_WIDTH), BF16),
        scratch_shapes=[pltpu.VMEM((SEQ, 2 * HEAD_DIM), BF16)] * 3,
        compiler_params=_params(("parallel", "parallel")),
        name="moba_prompt",
    )(zq, kp, vp)


def _head_diag(x):
    return jnp.concatenate(
        [x[h * T_PAD:(h + 1) * T_PAD, h * HEAD_DIM:(h + 1) * HEAD_DIM] for h in range(N_HEADS)], axis=1)


def _heads_to_lanes(ref, fn):
    return jnp.concatenate([fn(ref[h]) for h in range(N_HEADS)], axis=1)


BLOCKS_PER_STEP = 4
PAGES_PER_STEP = BLOCKS_PER_STEP * PAGES_PER_BLOCK
STEPS_PER_SEQ = N_PAST_BLOCKS // BLOCKS_PER_STEP
assert N_PAST_BLOCKS % BLOCKS_PER_STEP == 0


def _moba_sample_kernel(pt_ref, q_ref, kn_ref, vn_ref, hm_ref, *refs):
    del pt_ref
    k_refs, v_refs = refs[:PAGES_PER_STEP], refs[PAGES_PER_STEP:2 * PAGES_PER_STEP]
    o_ref, qf_ref, qb_ref, kmean_ref, m_ref, l_ref, part_ref = refs[2 * PAGES_PER_STEP:]
    step = pl.program_id(1)
    part_row = lax.broadcasted_iota(jnp.int32, (N_PART, QCOLS), 0)

    @pl.when(step == 0)
    def _():
        q8 = q_ref[...]
        qrep = jnp.broadcast_to(q8[None], (QCOLS // T_PAD, T_PAD, MIX_WIDTH)).reshape(QCOLS, MIX_WIDTH)
        qbd = qrep * hm_ref[...]
        qf_ref[...] = qbd
        qb_ref[...] = qbd.astype(BF16)
        kmean_ref[...] = jnp.zeros_like(kmean_ref)
        m_ref[...] = jnp.zeros_like(m_ref)
        l_ref[...] = jnp.zeros_like(l_ref)

    qb = qb_ref[...]

    def scores_t(k_bf):
        return lax.dot_general(k_bf, qb, _NT, preferred_element_type=F32) * SCALE

    to_bf = lambda x: x.astype(BF16)
    col_sum = lambda x: jnp.sum(x, axis=0, keepdims=True)

    def block_rows(page_refs, b, fn):
        pages = page_refs[b * PAGES_PER_BLOCK:(b + 1) * PAGES_PER_BLOCK]
        return jnp.concatenate([_heads_to_lanes(r, fn) for r in pages], axis=0)

    s_all = scores_t(jnp.concatenate([block_rows(k_refs, b, to_bf) for b in range(BLOCKS_PER_STEP)], axis=0))
    mean_row = lax.broadcasted_iota(jnp.int32, kmean_ref.shape, 0)
    m_new, l_new, kmean_new = m_ref[...], l_ref[...], kmean_ref[...]
    for b in range(BLOCKS_PER_STEP):
        n = step * BLOCKS_PER_STEP + b
        s = s_all[b * MOBA_BLOCK:(b + 1) * MOBA_BLOCK]
        mn = jnp.max(s, axis=0, keepdims=True)
        p = jnp.exp(s - mn)
        ln = jnp.sum(p, axis=0, keepdims=True)
        pv = jnp.dot(p.T.astype(BF16), block_rows(v_refs, b, to_bf), preferred_element_type=F32)
        part_ref[n] = _head_diag(pv)
        m_new = jnp.where(part_row == n, mn, m_new)
        l_new = jnp.where(part_row == n, ln, l_new)
        ksum = sum(_heads_to_lanes(r, col_sum)
                   for r in k_refs[b * PAGES_PER_BLOCK:(b + 1) * PAGES_PER_BLOCK])
        kmean_new = jnp.where(mean_row == n, ksum * (1.0 / MOBA_BLOCK), kmean_new)
    m_ref[...] = m_new
    l_ref[...] = l_new
    kmean_ref[...] = kmean_new

    @pl.when(step == STEPS_PER_SEQ - 1)
    def _():
        tail = jnp.zeros((128 - T_PAD, MIX_WIDTH), F32)
        knb = jnp.concatenate([kn_ref[...], tail], axis=0).astype(BF16)
        vnb = jnp.concatenate([vn_ref[...], tail], axis=0).astype(BF16)
        sc = scores_t(knb)
        key_t = lax.broadcasted_iota(jnp.int32, sc.shape, 0)
        qry_t = lax.broadcasted_iota(jnp.int32, sc.shape, 1) % T_PAD
        sc = jnp.where(key_t <= jnp.minimum(qry_t, DEC_SEQ - 1), sc, -jnp.inf)
        mc = jnp.max(sc, axis=0, keepdims=True)
        pc = jnp.exp(sc - mc)
        lc = jnp.sum(pc, axis=0, keepdims=True)
        oc = _head_diag(jnp.dot(pc.T.astype(BF16), vnb, preferred_element_type=F32))
        m_all = jnp.where(part_row == N_PAST_BLOCKS, mc, m_ref[...])
        l_all = jnp.where(part_row == N_PAST_BLOCKS, lc, l_ref[...])

        gate = _dot_nt_3pass(kmean_ref[...], qf_ref[...])
        sel = _top3_mask(gate, part_row < N_PAST_BLOCKS, axis=0)
        sel = jnp.where(part_row == N_PAST_BLOCKS, 1.0, sel)
        m_tot = jnp.max(jnp.where(sel > 0.5, m_all, -jnp.inf), axis=0, keepdims=True)
        w = jnp.where(sel > 0.5, jnp.exp(m_all - m_tot), 0.0)
        w = w * (1.0 / jnp.sum(w * l_all, axis=0, keepdims=True))
        w_cols = jnp.concatenate([w, jnp.zeros((QCOLS - N_PART, QCOLS), F32)], axis=0).T
        for h in range(N_HEADS):
            rows = slice(h * T_PAD, (h + 1) * T_PAD)
            cols = slice(h * HEAD_DIM, (h + 1) * HEAD_DIM)
            acc = w_cols[rows, N_PAST_BLOCKS:N_PAST_BLOCKS + 1] * oc[:, cols]
            for b in range(N_PAST_BLOCKS):
                acc += w_cols[rows, b:b + 1] * part_ref[b, :, cols]
            o_ref[:, cols] = acc


def _moba_sample(q8, kn8, vn8, cache_k, cache_v, pt_flat):
    head_of_col = jnp.arange(MIX_WIDTH, dtype=jnp.int32) // HEAD_DIM
    head_of_row = jnp.arange(QCOLS, dtype=jnp.int32) // T_PAD
    head_mask = (head_of_row[:, None] == head_of_col[None, :]).astype(F32)

    def tok_spec():
        return pl.BlockSpec((None, T_PAD, MIX_WIDTH), lambda s, n, pt: (s, 0, 0))

    def page_spec(j):
        return pl.BlockSpec((None, N_HEADS, PAGE_SIZE, HEAD_DIM),
                            lambda s, n, pt: (pt[s * N_PAGES + n * PAGES_PER_STEP + j], 0, 0, 0))

    page_specs = [page_spec(j) for j in range(PAGES_PER_STEP)]
    grid_spec = pltpu.PrefetchScalarGridSpec(
        num_scalar_prefetch=1,
        grid=(DEC_BATCH, STEPS_PER_SEQ),
        in_specs=[
            tok_spec(), tok_spec(), tok_spec(),
            pl.BlockSpec((QCOLS, MIX_WIDTH), lambda s, n, pt: (0, 0)),
        ] + page_specs + page_specs,
        out_specs=pl.BlockSpec((None, T_PAD, MIX_WIDTH), lambda s, n, pt: (s, 0, 0)),
        scratch_shapes=[
            pltpu.VMEM((QCOLS, MIX_WIDTH), F32),
            pltpu.VMEM((QCOLS, MIX_WIDTH), BF16),
            pltpu.VMEM((N_PART, MIX_WIDTH), F32),
            pltpu.VMEM((N_PART, QCOLS), F32),
            pltpu.VMEM((N_PART, QCOLS), F32),
            pltpu.VMEM((N_PAST_BLOCKS, T_PAD, MIX_WIDTH), F32),
        ],
    )
    return pl.pallas_call(
        _moba_sample_kernel,
        grid_spec=grid_spec,
        out_shape=jax.ShapeDtypeStruct((DEC_BATCH, T_PAD, MIX_WIDTH), F32),
        compiler_params=_params(("parallel", "arbitrary")),
        name="moba_sample",
    )(pt_flat, q8, kn8, vn8, head_mask, *([cache_k] * PAGES_PER_STEP), *([cache_v] * PAGES_PER_STEP))


def _rope_tables():
    half = ROT_DIM // 2
    inv = ROPE_THETA ** (-(jnp.arange(half, dtype=F32) * 2.0) / ROT_DIM)
    pos = jnp.concatenate([
        jnp.tile(jnp.arange(SEQ, dtype=jnp.int32), BATCH),
        jnp.tile(PAST_LEN + jnp.arange(DEC_SEQ, dtype=jnp.int32), DEC_BATCH),
    ])
    ang = pos.astype(F32)[:, None] * inv[None, :]
    cos, sin = jnp.cos(ang), jnp.sin(ang)
    zeros = jnp.zeros_like(cos)
    pad = jnp.zeros((M_ALL, HEAD_DIM - ROT_DIM), F32)
    c = jnp.concatenate([cos, cos, pad + 1.0], axis=1)
    s1 = jnp.concatenate([-sin, zeros, pad], axis=1)
    s2 = jnp.concatenate([zeros, sin, pad], axis=1)
    return c, s1, s2


def _pad_tokens(rows):
    x = rows.reshape(DEC_BATCH, DEC_SEQ, rows.shape[-1])
    return jnp.pad(x, ((0, 0), (0, T_PAD - DEC_SEQ), (0, 0)))


def _unpad_tokens(x):
    return x[:, :DEC_SEQ].reshape(M_SAMPLE, x.shape[-1])


def kernel(x_prompt, x_sample, cache_k, cache_v, cache_mem_k, cache_mem_v, page_table, mem_prompt,
           g_mix, w_in_a, w_in_b, g_v, w_s, b_s, w_out, g_mlp, w_up, w_down, g_mem, w_mem_kv,
           g_kv, w_kv, g_final):
    depth = g_mix.shape[0]
    assert depth == 2 and w_in_a.shape[0] == 1 and w_in_b.shape[0] == 1

    x0 = (x_prompt.reshape(M_PROMPT, D_MODEL), x_sample.reshape(M_SAMPLE, D_MODEL))
    rope = _rope_tables()

    mem_rows = N_MEM * N_MEM_HEADS
    mem_k_p, mem_v_p = _mem_kv(mem_prompt.reshape(BATCH * N_MEM, D_MODEL), g_mem, w_mem_kv.astype(BF16))
    mem_k_s = cache_mem_k.reshape(depth, DEC_BATCH, mem_rows, HEAD_DIM)
    mem_v_s = cache_mem_v.reshape(depth, DEC_BATCH, mem_rows, HEAD_DIM)

    w_out_b, w_up_b, w_down_b = w_out.astype(BF16), w_up.astype(BF16), w_down.astype(BF16)
    n_in_a = 2 * MIX_WIDTH + MEM_WIDTH
    z = _norm_matmul(x0, g_mix[0], w_in_a[0].astype(BF16), tn=n_in_a // 2)
    tril = jnp.tril(jnp.ones((CHUNK, CHUNK), bool))
    wmix_p = jnp.where(tril[None], w_s[0], 0.0)
    w_small = jnp.where(tril[None, :DEC_SEQ, :DEC_SEQ], w_s[0][:, :DEC_SEQ, :DEC_SEQ], 0.0)
    n_rep = CHUNK // DEC_SEQ
    wmix_s = jnp.einsum('ab,gts->gatbs', jnp.eye(n_rep, dtype=F32), w_small).reshape(N_GROUPS, CHUNK, CHUNK)
    wmix = jnp.stack([wmix_p, wmix_s]).astype(BF16)
    bias = jnp.stack([b_s[0].T, jnp.tile(b_s[0][:, :DEC_SEQ].T, (n_rep, 1))])
    mix, v_rows = _gmlp(z, g_v[0], wmix, bias)
    q_blk = 2 * MIX_WIDTH // MEM_WIDTH
    mo = (_mem_attend_prompt(z, q_blk, mem_k_p, mem_v_p, 0),
          _unpad_tokens(_mem_attend_sample(_pad_tokens(z[M_PROMPT:, 2 * MIX_WIDTH:]), mem_k_s, mem_v_s, 0)))
    x = _out_proj(x0, (mix, mix[M_PROMPT:]), mo, w_out_b, 0)
    x = _mlp(x, g_mlp[0], w_up_b, w_down_b, 0)

    kp, vp, kvs = _kv_proj(x, g_kv, w_kv.astype(BF16), rope)
    zq = _norm_matmul(x, g_mix[1], w_in_b[0].astype(BF16), tn=D_MODEL, rope=rope, n_rope_cols=MIX_WIDTH)
    moba_p = _moba_prompt(zq, kp, vp)
    moba_s = _moba_sample(
        _pad_tokens(zq[M_PROMPT:, :MIX_WIDTH]),
        _pad_tokens(kvs[:, :MIX_WIDTH]),
        _pad_tokens(kvs[:, MIX_WIDTH:]),
        jnp.transpose(cache_k, (0, 2, 1, 3)),
        jnp.transpose(cache_v, (0, 2, 1, 3)),
        page_table.reshape(-1),
    )
    moba_s = _unpad_tokens(moba_s)
    q_blk = MIX_WIDTH // MEM_WIDTH
    mo = (_mem_attend_prompt(zq, q_blk, mem_k_p, mem_v_p, 1),
          _unpad_tokens(_mem_attend_sample(_pad_tokens(zq[M_PROMPT:, MIX_WIDTH:]), mem_k_s, mem_v_s, 1)))
    x = _out_proj(x, (moba_p, moba_s), mo, w_out_b, 1)
    y_p, y_s = _mlp(x, g_mlp[1], w_up_b, w_down_b, 1, g_final=g_final)

    y_prompt = y_p.reshape(BATCH, SEQ, D_MODEL)
    y_sample = y_s.reshape(DEC_BATCH, DEC_SEQ, D_MODEL)
    k_prompt = jnp.transpose(kp, (0, 2, 1, 3))
    v_prompt = jnp.transpose(vp, (0, 2, 1, 3))
    k_sample = kvs[:, :MIX_WIDTH].reshape(DEC_BATCH, DEC_SEQ, N_HEADS, HEAD_DIM)
    v_sample = kvs[:, MIX_WIDTH:].reshape(DEC_BATCH, DEC_SEQ, N_HEADS, HEAD_DIM)
    mem_shape = (depth, BATCH, N_MEM, N_MEM_HEADS, HEAD_DIM)
    gmlp_v_sample = v_rows.reshape(1, DEC_BATCH, DEC_SEQ, MIX_WIDTH)
    return (y_prompt, y_sample, k_prompt, v_prompt, k_sample, v_sample,
            mem_k_p.reshape(mem_shape), mem_v_p.reshape(mem_shape), gmlp_v_sample)
```

```python
import functools
import math

import jax
import jax.numpy as jnp
import numpy as np
from jax import lax
from jax.experimental import pallas as pl
from jax.experimental.pallas import tpu as pltpu

F32 = jnp.float32
BF16 = jnp.bfloat16

D_MODEL = 2048
BATCH = 4
SEQ = 2048
DEC_BATCH = 128
DEC_SEQ = 4
PAST_LEN = 2048
PAGE_SIZE = 128
HEAD_DIM = 128
N_MEM_HEADS = 4
MEM_WIDTH = N_MEM_HEADS * HEAD_DIM
MIX_WIDTH = D_MODEL - MEM_WIDTH
N_HEADS = MIX_WIDTH // HEAD_DIM
N_GROUPS = MIX_WIDTH // HEAD_DIM
CHUNK = 128
D_FF = 4 * D_MODEL
N_MEM = 256
MOBA_BLOCK = 256
MOBA_TOPK = 3
ROPE_THETA = 500000.0
ROT_DIM = HEAD_DIM // 4
EPS = 1e-6
SCALE = HEAD_DIM ** -0.5

M_PROMPT = BATCH * SEQ
M_SAMPLE = DEC_BATCH * DEC_SEQ
M_ALL = M_PROMPT + M_SAMPLE
TM = M_SAMPLE
N_PROMPT_TILES = M_PROMPT // TM
N_BLOCKS = SEQ // MOBA_BLOCK
N_PAST_BLOCKS = PAST_LEN // MOBA_BLOCK
PAGES_PER_BLOCK = MOBA_BLOCK // PAGE_SIZE
N_PAGES = PAST_LEN // PAGE_SIZE
T_PAD = 8
QCOLS = 128
N_PART = 16

assert PAGES_PER_BLOCK == 2 and PAST_LEN % MOBA_BLOCK == 0 and N_PAST_BLOCKS >= MOBA_TOPK
assert N_HEADS * T_PAD <= QCOLS and N_PAST_BLOCKS < N_PART

VMEM_LIMIT = 56 * 1024 * 1024

EXP2_SCALE = SCALE * math.log2(math.e)
MASKED = -1e30

_NT = (((1,), (1,)), ((), ()))
_TN = (((0,), (0,)), ((), ()))


def _params(sem):
    return pltpu.CompilerParams(dimension_semantics=sem, vmem_limit_bytes=VMEM_LIMIT)


def _rms_unit(x):
    return x * lax.rsqrt(jnp.mean(x * x, axis=-1, keepdims=True) + EPS)


def _gelu(x):
    c = math.sqrt(2.0 / math.pi)
    return x * (0.5 * (1.0 + jnp.tanh(c * (x + 0.044715 * (x * x * x)))))


def _rotate(seg, c, s1, s2):
    half = ROT_DIM // 2
    return (seg * c + pltpu.roll(seg, HEAD_DIM - half, axis=1) * s1
            + pltpu.roll(seg, half, axis=1) * s2)


def _dot_nt_3pass(a, b):
    def split(x):
        hi = x.astype(BF16)
        return hi, (x - hi.astype(F32)).astype(BF16)

    def nt(x, y):
        return lax.dot_general(x, y, _NT, preferred_element_type=F32)

    (ah, al), (bh, bl) = split(a), split(b)
    return nt(ah, bh) + (nt(ah, bl) + nt(al, bh))


def _top3_mask(gate, valid, axis):
    n = gate.shape[axis]
    idx = lax.broadcasted_iota(jnp.int32, gate.shape, axis)
    g = jnp.where(valid, gate, -jnp.inf)
    sel = jnp.zeros(gate.shape, F32)
    for _ in range(MOBA_TOPK):
        m = jnp.max(g, axis=axis, keepdims=True)
        first = jnp.min(jnp.where(g == m, idx, n), axis=axis, keepdims=True)
        pick = idx == first
        sel = jnp.where(pick, 1.0, sel)
        g = jnp.where(pick, -jnp.inf, g)
    return jnp.where(valid, sel, 0.0)


def _rope_spec():
    tiles_per_seq = SEQ // TM
    return pl.BlockSpec((TM, HEAD_DIM),
                        lambda i, j: (jnp.where(i < N_PROMPT_TILES, i % tiles_per_seq, tiles_per_seq), 0))


def _split_specs(width, col_map=None):
    col = col_map or (lambda j: 0)
    return [
        pl.BlockSpec((TM, width), lambda i, j: (jnp.minimum(i, N_PROMPT_TILES - 1), col(j))),
        pl.BlockSpec((TM, width), lambda i, j: (0, col(j))),
    ]


def _norm_matmul_kernel(*refs, split_x, n_rope_heads):
    refs = list(refs)
    xs_ref = None
    x_ref = refs.pop(0)
    if split_x:
        xs_ref = refs.pop(0)
    g_ref, w_ref = refs.pop(0), refs.pop(0)
    if n_rope_heads:
        c_ref, s1_ref, s2_ref = refs.pop(0), refs.pop(0), refs.pop(0)
    o_ref, h_ref = refs
    i, j = pl.program_id(0), pl.program_id(1)

    def norm_from(ref):
        h_ref[...] = (_rms_unit(ref[...]) * g_ref[...]).astype(BF16)

    if split_x:
        pl.when((j == 0) & (i < N_PROMPT_TILES))(lambda: norm_from(x_ref))
        pl.when((j == 0) & (i >= N_PROMPT_TILES))(lambda: norm_from(xs_ref))
    else:
        pl.when(j == 0)(lambda: norm_from(x_ref))

    acc = jnp.dot(h_ref[...], w_ref[...], preferred_element_type=F32)

    if n_rope_heads:
        c, s1, s2 = c_ref[...], s1_ref[...], s2_ref[...]
        for hd in range(n_rope_heads):
            cols = slice(hd * HEAD_DIM, (hd + 1) * HEAD_DIM)
            o_ref[:, cols] = _rotate(acc[:, cols], c, s1, s2)
        o_ref[:, n_rope_heads * HEAD_DIM:] = acc[:, n_rope_heads * HEAD_DIM:]
    else:
        o_ref[...] = acc


def _norm_matmul(x, g, w, tn, rope=None, n_rope_cols=0):
    split_x = isinstance(x, tuple)
    k, n = w.shape
    assert n % tn == 0 and (n_rope_cols == 0 or tn == n)
    if split_x:
        in_specs, args = _split_specs(k), list(x)
    else:
        in_specs, args = [pl.BlockSpec((TM, k), lambda i, j: (i, 0))], [x]
    in_specs += [pl.BlockSpec((1, k), lambda i, j: (0, 0)), pl.BlockSpec((k, tn), lambda i, j: (0, j))]
    args += [g.reshape(1, k), w]
    if n_rope_cols:
        in_specs += [_rope_spec()] * 3
        args += list(rope)
    return pl.pallas_call(
        functools.partial(_norm_matmul_kernel, split_x=split_x, n_rope_heads=n_rope_cols // HEAD_DIM),
        grid=(M_ALL // TM, n // tn),
        in_specs=in_specs,
        out_specs=pl.BlockSpec((TM, tn), lambda i, j: (i, j)),
        out_shape=jax.ShapeDtypeStruct((M_ALL, n), F32),
        scratch_shapes=[pltpu.VMEM((TM, k), BF16)],
        compiler_params=_params(("parallel", "arbitrary")),
        name="norm_matmul",
    )(*args)


ROW_TILES_PER_SEQ = SEQ // TM


def _kv_proj_kernel(x_ref, g_ref, w_ref, c_ref, s1_ref, s2_ref, kp_ref, vp_ref, kvs_ref, h_ref):
    i, j = pl.program_id(0), pl.program_id(1)

    @pl.when(j == 0)
    def _():
        h_ref[...] = (_rms_unit(x_ref[...]) * g_ref[...]).astype(BF16)

    acc = jnp.dot(h_ref[...], w_ref[...], preferred_element_type=F32)
    is_prompt = i < N_PROMPT_TILES

    def heads(rotary):
        for hd in range(N_HEADS):
            seg = acc[:, hd * HEAD_DIM:(hd + 1) * HEAD_DIM]
            yield hd, (_rotate(seg, c_ref[...], s1_ref[...], s2_ref[...]) if rotary else seg)

    @pl.when(is_prompt & (j == 0))
    def _():
        for hd, seg in heads(True):
            kp_ref[hd] = seg

    @pl.when(is_prompt & (j == 1))
    def _():
        for hd, seg in heads(False):
            vp_ref[hd] = seg

    @pl.when(jnp.logical_not(is_prompt) & (j == 0))
    def _():
        for hd, seg in heads(True):
            kvs_ref[:, hd * HEAD_DIM:(hd + 1) * HEAD_DIM] = seg

    @pl.when(jnp.logical_not(is_prompt) & (j == 1))
    def _():
        kvs_ref[...] = acc


def _kv_proj(x, g, w, rope):
    k = x.shape[1]

    def prompt_map(i, j):
        ic = jnp.minimum(i, N_PROMPT_TILES - 1)
        return (ic // ROW_TILES_PER_SEQ, 0, ic % ROW_TILES_PER_SEQ, 0)

    head_block = (None, N_HEADS, TM, HEAD_DIM)
    return pl.pallas_call(
        _kv_proj_kernel,
        grid=(M_ALL // TM, 2),
        in_specs=[
            pl.BlockSpec((TM, k), lambda i, j: (i, 0)),
            pl.BlockSpec((1, k), lambda i, j: (0, 0)),
            pl.BlockSpec((k, MIX_WIDTH), lambda i, j: (0, j)),
        ] + [_rope_spec()] * 3,
        out_specs=[
            pl.BlockSpec(head_block, prompt_map),
            pl.BlockSpec(head_block, prompt_map),
            pl.BlockSpec((TM, MIX_WIDTH), lambda i, j: (0, jnp.where(i < N_PROMPT_TILES, 0, j))),
        ],
        out_shape=[
            jax.ShapeDtypeStruct((BATCH, N_HEADS, SEQ, HEAD_DIM), F32),
            jax.ShapeDtypeStruct((BATCH, N_HEADS, SEQ, HEAD_DIM), F32),
            jax.ShapeDtypeStruct((M_SAMPLE, 2 * MIX_WIDTH), F32),
        ],
        scratch_shapes=[pltpu.VMEM((TM, k), BF16)],
        compiler_params=_params(("arbitrary", "arbitrary")),
        name="kv_proj",
    )(x, g.reshape(1, k), w, *rope)


def _mem_kv_kernel(x_ref, g_ref, w_ref, k_ref, v_ref):
    h = (_rms_unit(x_ref[...]) * g_ref[...]).astype(BF16)
    acc = jnp.dot(h, w_ref[...], preferred_element_type=F32)
    rows = acc.shape[0]
    for hd in range(N_MEM_HEADS):
        k_ref[pl.ds(hd, rows, stride=N_MEM_HEADS), :] = acc[:, hd * HEAD_DIM:(hd + 1) * HEAD_DIM]
        v_ref[pl.ds(hd, rows, stride=N_MEM_HEADS), :] = acc[:, MEM_WIDTH + hd * HEAD_DIM:
                                                             MEM_WIDTH + (hd + 1) * HEAD_DIM]


def _mem_kv(mem_rows, g_mem, w_mem_kv):
    depth = g_mem.shape[0]
    out_block = pl.BlockSpec((None, None, N_MEM * N_MEM_HEADS, HEAD_DIM), lambda l, b: (l, b, 0, 0))
    out_shape = jax.ShapeDtypeStruct((depth, BATCH, N_MEM * N_MEM_HEADS, HEAD_DIM), F32)
    return pl.pallas_call(
        _mem_kv_kernel,
        grid=(depth, BATCH),
        in_specs=[
            pl.BlockSpec((N_MEM, D_MODEL), lambda l, b: (b, 0)),
            pl.BlockSpec((None, 1, D_MODEL), lambda l, b: (l, 0, 0)),
            pl.BlockSpec((None, D_MODEL, 2 * MEM_WIDTH), lambda l, b: (l, 0, 0)),
        ],
        out_specs=[out_block, out_block],
        out_shape=[out_shape, out_shape],
        compiler_params=_params(("parallel", "parallel")),
        name="mem_kv",
    )(mem_rows, g_mem.reshape(depth, 1, D_MODEL), w_mem_kv)


def _gmlp_kernel(z_ref, gv_ref, wmix_ref, bias_ref, mix_ref, v_ref):
    u = _gelu(z_ref[:, :MIX_WIDTH])
    v = _rms_unit(_gelu(z_ref[:, MIX_WIDTH:2 * MIX_WIDTH])) * gv_ref[...]
    v_ref[...] = v
    vb = v.astype(BF16)
    bias = bias_ref[...]
    for g in range(N_GROUPS):
        lo, hi = g * HEAD_DIM, (g + 1) * HEAD_DIM
        mixed = jnp.dot(wmix_ref[g], vb[:, lo:hi], preferred_element_type=F32)
        mix_ref[:, lo:hi] = (u[:, lo:hi] * (mixed + bias[:, g:g + 1])).astype(BF16)


def _gmlp(z, g_v, wmix, bias):
    n_prompt_tiles = M_PROMPT // CHUNK
    return pl.pallas_call(
        _gmlp_kernel,
        grid=(M_ALL // CHUNK,),
        in_specs=[
            pl.BlockSpec((CHUNK, z.shape[1]), lambda i: (i, 0)),
            pl.BlockSpec((1, MIX_WIDTH), lambda i: (0, 0)),
            pl.BlockSpec((None, N_GROUPS, CHUNK, CHUNK), lambda i: (i // n_prompt_tiles, 0, 0, 0)),
            pl.BlockSpec((None, CHUNK, N_GROUPS), lambda i: (i // n_prompt_tiles, 0, 0)),
        ],
        out_specs=[
            pl.BlockSpec((CHUNK, MIX_WIDTH), lambda i: (i, 0)),
            pl.BlockSpec((CHUNK, MIX_WIDTH), lambda i: (jnp.maximum(i - n_prompt_tiles, 0), 0)),
        ],
        out_shape=[
            jax.ShapeDtypeStruct((M_ALL, MIX_WIDTH), BF16),
            jax.ShapeDtypeStruct((M_SAMPLE, MIX_WIDTH), F32),
        ],
        compiler_params=_params(("arbitrary",)),
        name="gmlp",
    )(z, g_v.reshape(1, MIX_WIDTH), wmix, bias)


def _mem_attend_one(q, k_ref, v_ref, o_ref, rows):
    for h in range(N_MEM_HEADS):
        cols = slice(h * HEAD_DIM, (h + 1) * HEAD_DIM)
        k = k_ref[pl.ds(h, N_MEM, stride=N_MEM_HEADS), :].astype(BF16)
        v = v_ref[pl.ds(h, N_MEM, stride=N_MEM_HEADS), :].astype(BF16)
        s = lax.dot_general(q[:, cols].astype(BF16), k, _NT, preferred_element_type=F32) * SCALE
        p = jnp.exp(s - jnp.max(s, axis=-1, keepdims=True))
        l = jnp.sum(p, axis=-1, keepdims=True)
        o = jnp.dot(p.astype(BF16), v, preferred_element_type=F32)
        o_ref[rows, cols] = (o / l).astype(o_ref.dtype)


def _mem_prompt_kernel(q_ref, k_ref, v_ref, o_ref):
    _mem_attend_one(q_ref[...], k_ref, v_ref, o_ref, slice(None))


def _mem_attend_prompt(z, q_col_block, mem_k, mem_v, layer, tq=512):
    nq = SEQ // tq
    kv_spec = pl.BlockSpec((None, None, N_MEM * N_MEM_HEADS, HEAD_DIM), lambda b, i: (layer, b, 0, 0))
    return pl.pallas_call(
        _mem_prompt_kernel,
        grid=(BATCH, nq),
        in_specs=[pl.BlockSpec((tq, MEM_WIDTH), lambda b, i: (b * nq + i, q_col_block)), kv_spec, kv_spec],
        out_specs=pl.BlockSpec((tq, MEM_WIDTH), lambda b, i: (b * nq + i, 0)),
        out_shape=jax.ShapeDtypeStruct((M_PROMPT, MEM_WIDTH), BF16),
        compiler_params=_params(("parallel", "parallel")),
        name="mem_attend_prompt",
    )(z, mem_k, mem_v)


def _mem_sample_kernel(q_ref, k_ref, v_ref, o_ref, *, n_seq):
    n_rows, n_cols = N_MEM_HEADS * T_PAD, N_MEM * N_MEM_HEADS
    same_head = (lax.broadcasted_iota(jnp.int32, (n_rows, n_cols), 1) % N_MEM_HEADS
                 == lax.broadcasted_iota(jnp.int32, (n_rows, n_cols), 0) // T_PAD)
    scores = []
    for s in range(n_seq):
        q8 = q_ref[s]
        q_all = jnp.concatenate([q8[:, h * HEAD_DIM:(h + 1) * HEAD_DIM] for h in range(N_MEM_HEADS)], axis=0)
        sc = lax.dot_general(q_all.astype(BF16), k_ref[s].astype(BF16), _NT,
                             preferred_element_type=F32) * SCALE
        scores.append(jnp.where(same_head, sc, -jnp.inf))
    probs = []
    for sc in scores:
        p = jnp.exp(sc - jnp.max(sc, axis=-1, keepdims=True))
        probs.append((p.astype(BF16), jnp.sum(p, axis=-1, keepdims=True)))
    for s, (p, l) in enumerate(probs):
        o = jnp.dot(p, v_ref[s].astype(BF16), preferred_element_type=F32) / l
        for h in range(N_MEM_HEADS):
            o_ref[s, :, h * HEAD_DIM:(h + 1) * HEAD_DIM] = o[h * T_PAD:(h + 1) * T_PAD]


def _mem_attend_sample(q8, mem_k, mem_v, layer, n_seq=8):
    kv_spec = pl.BlockSpec((None, n_seq, N_MEM * N_MEM_HEADS, HEAD_DIM), lambda i: (layer, i, 0, 0))
    tok_spec = pl.BlockSpec((n_seq, T_PAD, MEM_WIDTH), lambda i: (i, 0, 0))
    return pl.pallas_call(
        functools.partial(_mem_sample_kernel, n_seq=n_seq),
        grid=(DEC_BATCH // n_seq,),
        in_specs=[tok_spec, kv_spec, kv_spec],
        out_specs=tok_spec,
        out_shape=jax.ShapeDtypeStruct((DEC_BATCH, T_PAD, MEM_WIDTH), F32),
        compiler_params=_params(("parallel",)),
        name="mem_attend_sample",
    )(q8, mem_k, mem_v)


def _out_proj_kernel(*refs, split_x):
    if split_x:
        x_ref, xs_ref, mixp_ref, mixs_ref, mop_ref, mos_ref, w1_ref, w2_ref, o_ref = refs
    else:
        x_ref, mixp_ref, mixs_ref, mop_ref, mos_ref, w1_ref, w2_ref, o_ref = refs
        xs_ref = x_ref
    i = pl.program_id(0)

    def run(x_r, mix_r, mo_r):
        acc = jnp.dot(mix_r[...].astype(BF16), w1_ref[...], preferred_element_type=F32)
        acc += jnp.dot(mo_r[...].astype(BF16), w2_ref[...], preferred_element_type=F32)
        o_ref[...] = x_r[...] + acc

    pl.when(i < N_PROMPT_TILES)(lambda: run(x_ref, mixp_ref, mop_ref))
    pl.when(i >= N_PROMPT_TILES)(lambda: run(xs_ref, mixs_ref, mos_ref))


def _out_proj(x, mix, mo, w_out, tn=D_MODEL):
    split_x = isinstance(x, tuple)
    if split_x:
        in_specs, args = _split_specs(tn, lambda j: j), list(x)
    else:
        in_specs, args = [pl.BlockSpec((TM, tn), lambda i, j: (i, j))], [x]
    in_specs += _split_specs(MIX_WIDTH) + _split_specs(MEM_WIDTH)
    in_specs += [
        pl.BlockSpec((MIX_WIDTH, tn), lambda i, j: (0, j)),
        pl.BlockSpec((MEM_WIDTH, tn), lambda i, j: (MIX_WIDTH // MEM_WIDTH, j)),
    ]
    args += [mix[0], mix[1], mo[0], mo[1], w_out, w_out]
    return pl.pallas_call(
        functools.partial(_out_proj_kernel, split_x=split_x),
        grid=(M_ALL // TM, D_MODEL // tn),
        in_specs=in_specs,
        out_specs=pl.BlockSpec((TM, tn), lambda i, j: (i, j)),
        out_shape=jax.ShapeDtypeStruct((M_ALL, D_MODEL), F32),
        compiler_params=_params(("parallel", "parallel")),
        name="out_proj",
    )(*args)


CAST_CHUNKS = 128


def _mlp_kernel(*refs, final_norm, n_cast):
    refs = list(refs)
    x_ref, g_ref, wu_ref, wd_ref = refs[:4]
    del refs[:4]
    if final_norm:
        gf_ref = refs.pop(0)
    src_refs = refs[:n_cast]
    del refs[:n_cast]
    if final_norm:
        yp_ref, ys_ref = refs[:2]
        del refs[:2]
    else:
        o_ref = refs.pop(0)
    dst_refs = refs[:n_cast]
    h_ref, acc_ref = refs[n_cast:]
    i, f = pl.program_id(0), pl.program_id(1)
    last = f == pl.num_programs(1) - 1

    if n_cast:
        @pl.when(i * pl.num_programs(1) + f < CAST_CHUNKS)
        def _():
            for src, dst in zip(src_refs, dst_refs):
                dst[...] = src[...].astype(BF16)

    @pl.when(f == 0)
    def _():
        h_ref[...] = (_rms_unit(x_ref[...]) * g_ref[...]).astype(BF16)
        acc_ref[...] = jnp.zeros_like(acc_ref)

    a = jnp.maximum(jnp.dot(h_ref[...], wu_ref[...], preferred_element_type=F32), 0.0)
    acc_ref[...] += jnp.dot((a * a).astype(BF16), wd_ref[...], preferred_element_type=F32)

    if final_norm:
        def finish(ref):
            ref[...] = _rms_unit(x_ref[...] + acc_ref[...]) * gf_ref[...]

        pl.when(last & (i < N_PROMPT_TILES))(lambda: finish(yp_ref))
        pl.when(last & (i >= N_PROMPT_TILES))(lambda: finish(ys_ref))
    else:
        @pl.when(last)
        def _():
            o_ref[...] = x_ref[...] + acc_ref[...]


def _mlp(x, g, w_up, w_down, g_final=None, cast=(), tf=1024):
    d, dff = w_up.shape
    nf = dff // tf
    final_norm = g_final is not None
    in_specs = [
        pl.BlockSpec((TM, d), lambda i, f: (i, 0)),
        pl.BlockSpec((1, d), lambda i, f: (0, 0)),
        pl.BlockSpec((d, tf), lambda i, f: (0, f)),
        pl.BlockSpec((tf, d), lambda i, f: (f, 0)),
    ]
    args = [x, g.reshape(1, d), w_up, w_down]
    if final_norm:
        in_specs.append(pl.BlockSpec((1, d), lambda i, f: (0, 0)))
        args.append(g_final.reshape(1, d))
        out_specs = [
            pl.BlockSpec((TM, d), lambda i, f: (jnp.minimum(i, N_PROMPT_TILES - 1), 0)),
            pl.BlockSpec((TM, d), lambda i, f: (0, 0)),
        ]
        out_shape = [jax.ShapeDtypeStruct((M_PROMPT, d), F32), jax.ShapeDtypeStruct((M_SAMPLE, d), F32)]
    else:
        out_specs = [pl.BlockSpec((TM, d), lambda i, f: (i, 0))]
        out_shape = [jax.ShapeDtypeStruct((M_ALL, d), F32)]

    def chunk(i, f):
        return jnp.minimum(i * nf + f, CAST_CHUNKS - 1)

    for w, layer in cast:
        rows, cols = w.shape[-2:]
        step = rows // CAST_CHUNKS
        assert rows % CAST_CHUNKS == 0 and step % 16 == 0 and (M_ALL // TM) * nf >= CAST_CHUNKS
        if w.ndim == 3:
            in_specs.append(pl.BlockSpec((None, step, cols), lambda i, f, layer=layer: (layer, chunk(i, f), 0)))
        else:
            in_specs.append(pl.BlockSpec((step, cols), lambda i, f: (chunk(i, f), 0)))
        args.append(w)
        out_specs.append(pl.BlockSpec((step, cols), lambda i, f: (chunk(i, f), 0)))
        out_shape.append(jax.ShapeDtypeStruct((rows, cols), BF16))
    return pl.pallas_call(
        functools.partial(_mlp_kernel, final_norm=final_norm, n_cast=len(cast)),
        grid=(M_ALL // TM, nf),
        in_specs=in_specs,
        out_specs=out_specs,
        out_shape=out_shape,
        scratch_shapes=[pltpu.VMEM((TM, d), BF16), pltpu.VMEM((TM, d), F32)],
        compiler_params=_params(("arbitrary", "arbitrary")),
        name="mlp",
    )(*args)


def _moba_prompt_kernel(q_ref, k_ref, v_ref, o_ref, qa_ref, ka_ref, va_ref):
    lane = lax.broadcasted_iota(jnp.int32, (SEQ, HEAD_DIM), 1)
    row_blk = lax.broadcasted_iota(jnp.int32, (SEQ, HEAD_DIM), 0) // MOBA_BLOCK
    k = k_ref[...]
    q = q_ref[...]

    ka_ref[:, :HEAD_DIM] = k.astype(BF16)
    ka_ref[:, HEAD_DIM:] = (lane == row_blk).astype(BF16)
    va_ref[:, :HEAD_DIM] = v_ref[...].astype(BF16)
    va_ref[:, HEAD_DIM:] = (lane == 0).astype(BF16)

    kmean = jnp.mean(k.reshape(N_BLOCKS, MOBA_BLOCK, HEAD_DIM), axis=1)
    gate = _dot_nt_3pass(kmean, q)
    blk = lax.broadcasted_iota(jnp.int32, gate.shape, 0)
    q_blk = lax.broadcasted_iota(jnp.int32, gate.shape, 1) // MOBA_BLOCK
    sel = _top3_mask(gate, blk < q_blk, axis=0)
    eye = (lax.broadcasted_iota(jnp.int32, (N_BLOCKS, HEAD_DIM), 0)
           == lax.broadcasted_iota(jnp.int32, (N_BLOCKS, HEAD_DIM), 1)).astype(BF16)
    sel_cols = lax.dot_general(sel.astype(BF16), eye, _TN, preferred_element_type=F32)
    qa_ref[:, :HEAD_DIM] = q.astype(BF16)
    qa_ref[:, HEAD_DIM:] = jnp.where(sel_cols > 0.5, 0.0, MASKED).astype(BF16)

    causal = (lax.broadcasted_iota(jnp.int32, (MOBA_BLOCK, MOBA_BLOCK), 1)
              <= lax.broadcasted_iota(jnp.int32, (MOBA_BLOCK, MOBA_BLOCK), 0))
    for qi in range(N_BLOCKS):
        rows = slice(qi * MOBA_BLOCK, (qi + 1) * MOBA_BLOCK)
        past = slice(0, qi * MOBA_BLOCK)
        s_own = lax.dot_general(qa_ref[rows, :HEAD_DIM], ka_ref[rows, :HEAD_DIM], _NT,
                                preferred_element_type=F32) * EXP2_SCALE
        s_own = jnp.where(causal, s_own, MASKED)
        m = jnp.max(s_own, axis=-1, keepdims=True)
        if qi:
            s_past = lax.dot_general(qa_ref[rows, :], ka_ref[past, :], _NT,
                                     preferred_element_type=F32) * EXP2_SCALE
            m = jnp.maximum(m, jnp.max(s_past, axis=-1, keepdims=True))
        pv = jnp.dot(jnp.exp2(s_own - m).astype(BF16), va_ref[rows, :], preferred_element_type=F32)
        if qi:
            pv += jnp.dot(jnp.exp2(s_past - m).astype(BF16), va_ref[past, :], preferred_element_type=F32)
        o_ref[rows, :] = (pv[:, :HEAD_DIM] / pv[:, HEAD_DIM:HEAD_DIM + 1]).astype(BF16)


def _moba_prompt(zq, kp, vp):
    kv_spec = pl.BlockSpec((None, None, SEQ, HEAD_DIM), lambda b, h: (b, h, 0, 0))
    return pl.pallas_call(
        _moba_prompt_kernel,
        grid=(BATCH, N_HEADS),
        in_specs=[pl.BlockSpec((SEQ, HEAD_DIM), lambda b, h: (b, h)), kv_spec, kv_spec],
        out_specs=pl.BlockSpec((SEQ, HEAD_DIM), lambda b, h: (b, h)),
        out_shape=jax.ShapeDtypeStruct((M_PROMPT, MIX_WIDTH), BF16),
        scratch_shapes=[pltpu.VMEM((SEQ, 2 * HEAD_DIM), BF16)] * 3,
        compiler_params=_params(("parallel", "parallel")),
        name="moba_prompt",
    )(zq, kp, vp)


def _head_diag(x):
    return jnp.concatenate(
        [x[h * T_PAD:(h + 1) * T_PAD, h * HEAD_DIM:(h + 1) * HEAD_DIM] for h in range(N_HEADS)], axis=1)


BLOCKS_PER_STEP = 4
PAGES_PER_STEP = BLOCKS_PER_STEP * PAGES_PER_BLOCK
STEPS_PER_SEQ = N_PAST_BLOCKS // BLOCKS_PER_STEP
assert N_PAST_BLOCKS % BLOCKS_PER_STEP == 0


def _moba_sample_kernel(pt_ref, q_ref, kn_ref, vn_ref, hm_ref, *refs):
    del pt_ref
    k_refs, v_refs = refs[:PAGES_PER_STEP], refs[PAGES_PER_STEP:2 * PAGES_PER_STEP]
    o_ref, qf_ref, qb_ref, kmean_ref, m_ref, l_ref, part_ref = refs[2 * PAGES_PER_STEP:]
    step = pl.program_id(1)
    part_row = lax.broadcasted_iota(jnp.int32, (N_PART, QCOLS), 0)

    @pl.when(step == 0)
    def _():
        q8 = q_ref[...]
        qrep = jnp.broadcast_to(q8[None], (QCOLS // T_PAD, T_PAD, MIX_WIDTH)).reshape(QCOLS, MIX_WIDTH)
        qbd = qrep * hm_ref[...]
        qf_ref[...] = qbd
        qb_ref[...] = qbd.astype(BF16)
        kmean_ref[...] = jnp.zeros_like(kmean_ref)
        m_ref[...] = jnp.zeros_like(m_ref)
        l_ref[...] = jnp.zeros_like(l_ref)

    qb = qb_ref[...]

    def scores_t(k_bf):
        return lax.dot_general(k_bf, qb, _NT, preferred_element_type=F32) * SCALE

    def load_block(page_refs, b, with_sum):
        rows, total = [], None
        for r in page_refs[b * PAGES_PER_BLOCK:(b + 1) * PAGES_PER_BLOCK]:
            slabs = [r[h] for h in range(N_HEADS)]
            rows.append(jnp.concatenate([x.astype(BF16) for x in slabs], axis=1))
            if with_sum:
                part = jnp.concatenate([jnp.sum(x, axis=0, keepdims=True) for x in slabs], axis=1)
                total = part if total is None else total + part
        return jnp.concatenate(rows, axis=0), total

    k_blocks = [load_block(k_refs, b, True) for b in range(BLOCKS_PER_STEP)]
    s_all = scores_t(jnp.concatenate([kb for kb, _ in k_blocks], axis=0))
    mean_row = lax.broadcasted_iota(jnp.int32, kmean_ref.shape, 0)
    m_new, l_new, kmean_new = m_ref[...], l_ref[...], kmean_ref[...]
    for b in range(BLOCKS_PER_STEP):
        n = step * BLOCKS_PER_STEP + b
        s = s_all[b * MOBA_BLOCK:(b + 1) * MOBA_BLOCK]
        mn = jnp.max(s, axis=0, keepdims=True)
        p = jnp.exp(s - mn)
        ln = jnp.sum(p, axis=0, keepdims=True)
        pv = jnp.dot(p.T.astype(BF16), load_block(v_refs, b, False)[0], preferred_element_type=F32)
        part_ref[n] = _head_diag(pv)
        m_new = jnp.where(part_row == n, mn, m_new)
        l_new = jnp.where(part_row == n, ln, l_new)
        kmean_new = jnp.where(mean_row == n, k_blocks[b][1] * (1.0 / MOBA_BLOCK), kmean_new)
    m_ref[...] = m_new
    l_ref[...] = l_new
    kmean_ref[...] = kmean_new

    @pl.when(step == STEPS_PER_SEQ - 1)
    def _():
        tail = jnp.zeros((128 - T_PAD, MIX_WIDTH), F32)
        knb = jnp.concatenate([kn_ref[...], tail], axis=0).astype(BF16)
        vnb = jnp.concatenate([vn_ref[...], tail], axis=0).astype(BF16)
        sc = scores_t(knb)
        key_t = lax.broadcasted_iota(jnp.int32, sc.shape, 0)
        qry_t = lax.broadcasted_iota(jnp.int32, sc.shape, 1) % T_PAD
        sc = jnp.where(key_t <= jnp.minimum(qry_t, DEC_SEQ - 1), sc, -jnp.inf)
        mc = jnp.max(sc, axis=0, keepdims=True)
        pc = jnp.exp(sc - mc)
        lc = jnp.sum(pc, axis=0, keepdims=True)
        oc = _head_diag(jnp.dot(pc.T.astype(BF16), vnb, preferred_element_type=F32))
        m_all = jnp.where(part_row == N_PAST_BLOCKS, mc, m_ref[...])
        l_all = jnp.where(part_row == N_PAST_BLOCKS, lc, l_ref[...])

        gate = _dot_nt_3pass(kmean_ref[...], qf_ref[...])
        sel = _top3_mask(gate, part_row < N_PAST_BLOCKS, axis=0)
        sel = jnp.where(part_row == N_PAST_BLOCKS, 1.0, sel)
        m_tot = jnp.max(jnp.where(sel > 0.5, m_all, -jnp.inf), axis=0, keepdims=True)
        w = jnp.where(sel > 0.5, jnp.exp(m_all - m_tot), 0.0)
        w = w * (1.0 / jnp.sum(w * l_all, axis=0, keepdims=True))
        w_cols = jnp.concatenate([w, jnp.zeros((QCOLS - N_PART, QCOLS), F32)], axis=0).T
        for h in range(N_HEADS):
            rows = slice(h * T_PAD, (h + 1) * T_PAD)
            cols = slice(h * HEAD_DIM, (h + 1) * HEAD_DIM)
            acc = w_cols[rows, N_PAST_BLOCKS:N_PAST_BLOCKS + 1] * oc[:, cols]
            for b in range(N_PAST_BLOCKS):
                acc += w_cols[rows, b:b + 1] * part_ref[b, :, cols]
            o_ref[:, cols] = acc


def _moba_sample(q8, kn8, vn8, cache_k, cache_v, pt_flat):
    head_of_col = jnp.arange(MIX_WIDTH, dtype=jnp.int32) // HEAD_DIM
    head_of_row = jnp.arange(QCOLS, dtype=jnp.int32) // T_PAD
    head_mask = (head_of_row[:, None] == head_of_col[None, :]).astype(F32)

    def tok_spec():
        return pl.BlockSpec((None, T_PAD, MIX_WIDTH), lambda s, n, pt: (s, 0, 0))

    def page_spec(j):
        return pl.BlockSpec((None, N_HEADS, PAGE_SIZE, HEAD_DIM),
                            lambda s, n, pt: (pt[s * N_PAGES + n * PAGES_PER_STEP + j], 0, 0, 0))

    page_specs = [page_spec(j) for j in range(PAGES_PER_STEP)]
    grid_spec = pltpu.PrefetchScalarGridSpec(
        num_scalar_prefetch=1,
        grid=(DEC_BATCH, STEPS_PER_SEQ),
        in_specs=[
            tok_spec(), tok_spec(), tok_spec(),
            pl.BlockSpec((QCOLS, MIX_WIDTH), lambda s, n, pt: (0, 0)),
        ] + page_specs + page_specs,
        out_specs=pl.BlockSpec((None, T_PAD, MIX_WIDTH), lambda s, n, pt: (s, 0, 0)),
        scratch_shapes=[
            pltpu.VMEM((QCOLS, MIX_WIDTH), F32),
            pltpu.VMEM((QCOLS, MIX_WIDTH), BF16),
            pltpu.VMEM((N_PART, MIX_WIDTH), F32),
            pltpu.VMEM((N_PART, QCOLS), F32),
            pltpu.VMEM((N_PART, QCOLS), F32),
            pltpu.VMEM((N_PAST_BLOCKS, T_PAD, MIX_WIDTH), F32),
        ],
    )
    return pl.pallas_call(
        _moba_sample_kernel,
        grid_spec=grid_spec,
        out_shape=jax.ShapeDtypeStruct((DEC_BATCH, T_PAD, MIX_WIDTH), F32),
        compiler_params=_params(("parallel", "arbitrary")),
        name="moba_sample",
    )(pt_flat, q8, kn8, vn8, head_mask, *([cache_k] * PAGES_PER_STEP), *([cache_v] * PAGES_PER_STEP))


def _rope_tables():
    half = ROT_DIM // 2
    inv = ROPE_THETA ** (-(np.arange(half, dtype=np.float64) * 2.0) / ROT_DIM)
    pos = np.concatenate([np.arange(SEQ), PAST_LEN + np.arange(TM) % DEC_SEQ]).astype(np.float64)
    ang = pos[:, None] * inv[None, :]
    cos, sin = np.cos(ang), np.sin(ang)
    zeros = np.zeros_like(cos)
    pad = np.zeros((pos.shape[0], HEAD_DIM - ROT_DIM))
    c = np.concatenate([cos, cos, pad + 1.0], axis=1)
    s1 = np.concatenate([-sin, zeros, pad], axis=1)
    s2 = np.concatenate([zeros, sin, pad], axis=1)
    return tuple(jnp.asarray(t, dtype=F32) for t in (c, s1, s2))


def _pad_tokens(rows):
    x = rows.reshape(DEC_BATCH, DEC_SEQ, rows.shape[-1])
    return jnp.pad(x, ((0, 0), (0, T_PAD - DEC_SEQ), (0, 0)))


def _unpad_tokens(x):
    return x[:, :DEC_SEQ].reshape(M_SAMPLE, x.shape[-1])


def kernel(x_prompt, x_sample, cache_k, cache_v, cache_mem_k, cache_mem_v, page_table, mem_prompt,
           g_mix, w_in_a, w_in_b, g_v, w_s, b_s, w_out, g_mlp, w_up, w_down, g_mem, w_mem_kv,
           g_kv, w_kv, g_final):
    depth = g_mix.shape[0]
    assert depth == 2 and w_in_a.shape[0] == 1 and w_in_b.shape[0] == 1

    x0 = (x_prompt.reshape(M_PROMPT, D_MODEL), x_sample.reshape(M_SAMPLE, D_MODEL))
    rope = _rope_tables()

    mem_rows = N_MEM * N_MEM_HEADS
    mem_k_p, mem_v_p = _mem_kv(mem_prompt.reshape(BATCH * N_MEM, D_MODEL), g_mem, w_mem_kv.astype(BF16))
    mem_k_s = cache_mem_k.reshape(depth, DEC_BATCH, mem_rows, HEAD_DIM)
    mem_v_s = cache_mem_v.reshape(depth, DEC_BATCH, mem_rows, HEAD_DIM)

    n_in_a = 2 * MIX_WIDTH + MEM_WIDTH
    z = _norm_matmul(x0, g_mix[0], w_in_a[0].astype(BF16), tn=n_in_a // 2)
    tril = jnp.tril(jnp.ones((CHUNK, CHUNK), bool))
    wmix_p = jnp.where(tril[None], w_s[0], 0.0)
    w_small = jnp.where(tril[None, :DEC_SEQ, :DEC_SEQ], w_s[0][:, :DEC_SEQ, :DEC_SEQ], 0.0)
    n_rep = CHUNK // DEC_SEQ
    wmix_s = jnp.einsum('ab,gts->gatbs', jnp.eye(n_rep, dtype=F32), w_small).reshape(N_GROUPS, CHUNK, CHUNK)
    wmix = jnp.stack([wmix_p, wmix_s]).astype(BF16)
    bias = jnp.stack([b_s[0].T, jnp.tile(b_s[0][:, :DEC_SEQ].T, (n_rep, 1))])
    mix, v_rows = _gmlp(z, g_v[0], wmix, bias)
    q_blk = 2 * MIX_WIDTH // MEM_WIDTH
    mo = (_mem_attend_prompt(z, q_blk, mem_k_p, mem_v_p, 0),
          _unpad_tokens(_mem_attend_sample(_pad_tokens(z[M_PROMPT:, 2 * MIX_WIDTH:]), mem_k_s, mem_v_s, 0)))
    x = _out_proj(x0, (mix, mix[M_PROMPT:]), mo, w_out[0].astype(BF16))
    x, w_up1, w_down1, w_out1, w_kv_b, w_in_b1 = _mlp(
        x, g_mlp[0], w_up[0].astype(BF16), w_down[0].astype(BF16),
        cast=((w_up, 1), (w_down, 1), (w_out, 1), (w_kv, None), (w_in_b, 0)))

    kp, vp, kvs = _kv_proj(x, g_kv, w_kv_b, rope)
    zq = _norm_matmul(x, g_mix[1], w_in_b1, tn=D_MODEL, rope=rope, n_rope_cols=MIX_WIDTH)
    moba_p = _moba_prompt(zq, kp, vp)
    moba_s = _moba_sample(
        _pad_tokens(zq[M_PROMPT:, :MIX_WIDTH]),
        _pad_tokens(kvs[:, :MIX_WIDTH]),
        _pad_tokens(kvs[:, MIX_WIDTH:]),
        jnp.transpose(cache_k, (0, 2, 1, 3)),
        jnp.transpose(cache_v, (0, 2, 1, 3)),
        page_table.reshape(-1),
    )
    moba_s = _unpad_tokens(moba_s)
    q_blk = MIX_WIDTH // MEM_WIDTH
    mo = (_mem_attend_prompt(zq, q_blk, mem_k_p, mem_v_p, 1),
          _unpad_tokens(_mem_attend_sample(_pad_tokens(zq[M_PROMPT:, MIX_WIDTH:]), mem_k_s, mem_v_s, 1)))
    x = _out_proj(x, (moba_p, moba_s), mo, w_out1)
    y_p, y_s = _mlp(x, g_mlp[1], w_up1, w_down1, g_final=g_final)

    y_prompt = y_p.reshape(BATCH, SEQ, D_MODEL)
    y_sample = y_s.reshape(DEC_BATCH, DEC_SEQ, D_MODEL)
    k_prompt = jnp.transpose(kp, (0, 2, 1, 3))
    v_prompt = jnp.transpose(vp, (0, 2, 1, 3))
    k_sample = kvs[:, :MIX_WIDTH].reshape(DEC_BATCH, DEC_SEQ, N_HEADS, HEAD_DIM)
    v_sample = kvs[:, MIX_WIDTH:].reshape(DEC_BATCH, DEC_SEQ, N_HEADS, HEAD_DIM)
    mem_shape = (depth, BATCH, N_MEM, N_MEM_HEADS, HEAD_DIM)
    gmlp_v_sample = v_rows.reshape(1, DEC_BATCH, DEC_SEQ, MIX_WIDTH)
    return (y_prompt, y_sample, k_prompt, v_prompt, k_sample, v_sample,
            mem_k_p.reshape(mem_shape), mem_v_p.reshape(mem_shape), gmlp_v_sample)
```

```python
import functools
import math
from typing import NamedTuple

import jax
import jax.numpy as jnp
import numpy as np
from jax import lax
from jax.experimental import pallas as pl
from jax.experimental.pallas import tpu as pltpu

F32 = jnp.float32
BF16 = jnp.bfloat16

D_MODEL = 2048
BATCH = 4
SEQ = 2048
DEC_BATCH = 128
DEC_SEQ = 4
PAST_LEN = 2048
PAGE_SIZE = 128
HEAD_DIM = 128
N_MEM_HEADS = 4
MEM_WIDTH = N_MEM_HEADS * HEAD_DIM
MIX_WIDTH = D_MODEL - MEM_WIDTH
N_HEADS = MIX_WIDTH // HEAD_DIM
N_GROUPS = MIX_WIDTH // HEAD_DIM
CHUNK = 128
D_FF = 4 * D_MODEL
N_MEM = 256
MOBA_BLOCK = 256
MOBA_TOPK = 3
ROPE_THETA = 500000.0
ROT_DIM = HEAD_DIM // 4
EPS = 1e-6
SCALE = HEAD_DIM ** -0.5

M_PROMPT = BATCH * SEQ
M_SAMPLE = DEC_BATCH * DEC_SEQ
M_ALL = M_PROMPT + M_SAMPLE
TM = M_SAMPLE
N_PROMPT_TILES = M_PROMPT // TM
N_BLOCKS = SEQ // MOBA_BLOCK
N_PAST_BLOCKS = PAST_LEN // MOBA_BLOCK
PAGES_PER_BLOCK = MOBA_BLOCK // PAGE_SIZE
N_PAGES = PAST_LEN // PAGE_SIZE
T_PAD = 8
QCOLS = 128
N_PART = 16

assert PAGES_PER_BLOCK == 2 and PAST_LEN % MOBA_BLOCK == 0 and N_PAST_BLOCKS >= MOBA_TOPK
assert N_HEADS * T_PAD <= QCOLS and N_PAST_BLOCKS < N_PART

VMEM_LIMIT = 56 * 1024 * 1024

EXP2_SCALE = SCALE * math.log2(math.e)
MASKED = -1e30

_NT = (((1,), (1,)), ((), ()))
_TN = (((0,), (0,)), ((), ()))


def _params(sem):
    return pltpu.CompilerParams(dimension_semantics=sem, vmem_limit_bytes=VMEM_LIMIT)


def _rms_unit(x):
    return x * lax.rsqrt(jnp.mean(x * x, axis=-1, keepdims=True) + EPS)


def _gelu(x):
    c = math.sqrt(2.0 / math.pi)
    return x * (0.5 * (1.0 + jnp.tanh(c * (x + 0.044715 * (x * x * x)))))


def _rotate(seg, c, s1, s2):
    half = ROT_DIM // 2
    return (seg * c + pltpu.roll(seg, HEAD_DIM - half, axis=1) * s1
            + pltpu.roll(seg, half, axis=1) * s2)


def _dot_nt_3pass(a, b):
    def split(x):
        hi = x.astype(BF16)
        return hi, (x - hi.astype(F32)).astype(BF16)

    def nt(x, y):
        return lax.dot_general(x, y, _NT, preferred_element_type=F32)

    (ah, al), (bh, bl) = split(a), split(b)
    return nt(ah, bh) + (nt(ah, bl) + nt(al, bh))


def _top3_mask(gate, valid, axis):
    n = gate.shape[axis]
    idx = lax.broadcasted_iota(jnp.int32, gate.shape, axis)
    g = jnp.where(valid, gate, -jnp.inf)
    sel = jnp.zeros(gate.shape, F32)
    for _ in range(MOBA_TOPK):
        m = jnp.max(g, axis=axis, keepdims=True)
        first = jnp.min(jnp.where(g == m, idx, n), axis=axis, keepdims=True)
        pick = idx == first
        sel = jnp.where(pick, 1.0, sel)
        g = jnp.where(pick, -jnp.inf, g)
    return jnp.where(valid, sel, 0.0)


def _rope_spec():
    tiles_per_seq = SEQ // TM
    return pl.BlockSpec((TM, HEAD_DIM),
                        lambda i, j: (jnp.where(i < N_PROMPT_TILES, i % tiles_per_seq, tiles_per_seq), 0))


class _CastPlumbing(NamedTuple):
    in_specs: list
    out_specs: list
    out_shape: list
    args: list


def _cast_plumbing(cast, n_chunks, linear_step):
    def chunk(*ids):
        return jnp.minimum(linear_step(*ids), n_chunks - 1)

    plumbing = _CastPlumbing([], [], [], [])
    for w, layer in cast:
        rows, cols = w.shape[-2:]
        step = rows // n_chunks
        assert rows % n_chunks == 0 and step % 16 == 0
        if w.ndim == 3:
            plumbing.in_specs.append(
                pl.BlockSpec((None, step, cols), lambda *ids, layer=layer: (layer, chunk(*ids), 0)))
        else:
            plumbing.in_specs.append(pl.BlockSpec((step, cols), lambda *ids: (chunk(*ids), 0)))
        plumbing.out_specs.append(pl.BlockSpec((step, cols), lambda *ids: (chunk(*ids), 0)))
        plumbing.out_shape.append(jax.ShapeDtypeStruct((rows, cols), BF16))
        plumbing.args.append(w)
    return plumbing


def _ride_casts(src_refs, dst_refs, step, n_chunks):
    if not src_refs:
        return

    @pl.when(step < n_chunks)
    def _():
        for src, dst in zip(src_refs, dst_refs):
            dst[...] = src[...].astype(BF16)


def _split_specs(width, col_map=None):
    col = col_map or (lambda j: 0)
    return [
        pl.BlockSpec((TM, width), lambda i, j: (jnp.minimum(i, N_PROMPT_TILES - 1), col(j))),
        pl.BlockSpec((TM, width), lambda i, j: (0, col(j))),
    ]


def _norm_matmul_kernel(*refs, split_x, n_rope_heads):
    refs = list(refs)
    xs_ref = None
    x_ref = refs.pop(0)
    if split_x:
        xs_ref = refs.pop(0)
    g_ref, w_ref = refs.pop(0), refs.pop(0)
    if n_rope_heads:
        c_ref, s1_ref, s2_ref = refs.pop(0), refs.pop(0), refs.pop(0)
    o_ref, h_ref = refs
    i, j = pl.program_id(0), pl.program_id(1)

    def norm_from(ref):
        h_ref[...] = (_rms_unit(ref[...]) * g_ref[...]).astype(BF16)

    if split_x:
        pl.when((j == 0) & (i < N_PROMPT_TILES))(lambda: norm_from(x_ref))
        pl.when((j == 0) & (i >= N_PROMPT_TILES))(lambda: norm_from(xs_ref))
    else:
        pl.when(j == 0)(lambda: norm_from(x_ref))

    acc = jnp.dot(h_ref[...], w_ref[...], preferred_element_type=F32)

    if n_rope_heads:
        c, s1, s2 = c_ref[...], s1_ref[...], s2_ref[...]
        for hd in range(n_rope_heads):
            cols = slice(hd * HEAD_DIM, (hd + 1) * HEAD_DIM)
            o_ref[:, cols] = _rotate(acc[:, cols], c, s1, s2)
        o_ref[:, n_rope_heads * HEAD_DIM:] = acc[:, n_rope_heads * HEAD_DIM:]
    else:
        o_ref[...] = acc


def _norm_matmul(x, g, w, tn, rope=None, n_rope_cols=0):
    split_x = isinstance(x, tuple)
    k, n = w.shape
    assert n % tn == 0 and (n_rope_cols == 0 or tn == n)
    if split_x:
        in_specs, args = _split_specs(k), list(x)
    else:
        in_specs, args = [pl.BlockSpec((TM, k), lambda i, j: (i, 0))], [x]
    in_specs += [pl.BlockSpec((1, k), lambda i, j: (0, 0)), pl.BlockSpec((k, tn), lambda i, j: (0, j))]
    args += [g.reshape(1, k), w]
    if n_rope_cols:
        in_specs += [_rope_spec()] * 3
        args += list(rope)
    return pl.pallas_call(
        functools.partial(_norm_matmul_kernel, split_x=split_x, n_rope_heads=n_rope_cols // HEAD_DIM),
        grid=(M_ALL // TM, n // tn),
        in_specs=in_specs,
        out_specs=pl.BlockSpec((TM, tn), lambda i, j: (i, j)),
        out_shape=jax.ShapeDtypeStruct((M_ALL, n), F32),
        scratch_shapes=[pltpu.VMEM((TM, k), BF16)],
        compiler_params=_params(("parallel", "arbitrary")),
        name="norm_matmul",
    )(*args)


ROW_TILES_PER_SEQ = SEQ // TM


def _kv_proj_kernel(x_ref, g_ref, w_ref, c_ref, s1_ref, s2_ref, *out_refs, head_major):
    h = (_rms_unit(x_ref[...]) * g_ref[...]).astype(BF16)
    acc = jnp.dot(h, w_ref[...], preferred_element_type=F32)
    c, s1, s2 = c_ref[...], s1_ref[...], s2_ref[...]
    for hd in range(N_HEADS):
        k_cols = slice(hd * HEAD_DIM, (hd + 1) * HEAD_DIM)
        v_cols = slice(MIX_WIDTH + hd * HEAD_DIM, MIX_WIDTH + (hd + 1) * HEAD_DIM)
        k_seg = _rotate(acc[:, k_cols], c, s1, s2)
        if head_major:
            kp_ref, vp_ref = out_refs
            kp_ref[hd] = k_seg
            vp_ref[hd] = acc[:, v_cols]
        else:
            (kv_ref,) = out_refs
            kv_ref[:, k_cols] = k_seg
            kv_ref[:, v_cols] = acc[:, v_cols]


def _kv_proj(x, g, w, rope):
    k, n = w.shape
    common = [
        pl.BlockSpec((1, k), lambda i: (0, 0)),
        pl.BlockSpec((k, n), lambda i: (0, 0), pipeline_mode=pl.Buffered(1)),
    ]
    args = (g.reshape(1, k), w) + tuple(rope)
    head_block = pl.BlockSpec((None, N_HEADS, TM, HEAD_DIM),
                              lambda i: (i // ROW_TILES_PER_SEQ, 0, i % ROW_TILES_PER_SEQ, 0))
    head_shape = jax.ShapeDtypeStruct((BATCH, N_HEADS, SEQ, HEAD_DIM), F32)
    kp, vp = pl.pallas_call(
        functools.partial(_kv_proj_kernel, head_major=True),
        grid=(N_PROMPT_TILES,),
        in_specs=[pl.BlockSpec((TM, k), lambda i: (i, 0))] + common
        + [pl.BlockSpec((TM, HEAD_DIM), lambda i: (i % ROW_TILES_PER_SEQ, 0))] * 3,
        out_specs=[head_block, head_block],
        out_shape=[head_shape, head_shape],
        compiler_params=_params(("parallel",)),
        name="kv_proj_prompt",
    )(x, *args)
    kvs = pl.pallas_call(
        functools.partial(_kv_proj_kernel, head_major=False),
        grid=(1,),
        in_specs=[pl.BlockSpec((TM, k), lambda i: (N_PROMPT_TILES, 0))] + common
        + [pl.BlockSpec((TM, HEAD_DIM), lambda i: (ROW_TILES_PER_SEQ, 0))] * 3,
        out_specs=pl.BlockSpec((TM, n), lambda i: (0, 0)),
        out_shape=jax.ShapeDtypeStruct((M_SAMPLE, n), F32),
        compiler_params=_params(("arbitrary",)),
        name="kv_proj_sample",
    )(x, *args)
    return kp, vp, kvs


def _mem_kv_kernel(x_ref, g_ref, w_ref, k_ref, v_ref):
    h = (_rms_unit(x_ref[...]) * g_ref[...]).astype(BF16)
    acc = jnp.dot(h, w_ref[...], preferred_element_type=F32)
    rows = acc.shape[0]
    for hd in range(N_MEM_HEADS):
        k_ref[pl.ds(hd, rows, stride=N_MEM_HEADS), :] = acc[:, hd * HEAD_DIM:(hd + 1) * HEAD_DIM]
        v_ref[pl.ds(hd, rows, stride=N_MEM_HEADS), :] = acc[:, MEM_WIDTH + hd * HEAD_DIM:
                                                             MEM_WIDTH + (hd + 1) * HEAD_DIM]


def _mem_kv(mem_rows, g_mem, w_mem_kv):
    depth = g_mem.shape[0]
    out_block = pl.BlockSpec((None, None, N_MEM * N_MEM_HEADS, HEAD_DIM), lambda l, b: (l, b, 0, 0))
    out_shape = jax.ShapeDtypeStruct((depth, BATCH, N_MEM * N_MEM_HEADS, HEAD_DIM), F32)
    return pl.pallas_call(
        _mem_kv_kernel,
        grid=(depth, BATCH),
        in_specs=[
            pl.BlockSpec((N_MEM, D_MODEL), lambda l, b: (b, 0)),
            pl.BlockSpec((None, 1, D_MODEL), lambda l, b: (l, 0, 0)),
            pl.BlockSpec((None, D_MODEL, 2 * MEM_WIDTH), lambda l, b: (l, 0, 0)),
        ],
        out_specs=[out_block, out_block],
        out_shape=[out_shape, out_shape],
        compiler_params=_params(("parallel", "parallel")),
        name="mem_kv",
    )(mem_rows, g_mem.reshape(depth, 1, D_MODEL), w_mem_kv)


GMLP_CAST_CHUNKS = 64


def _gmlp_kernel(z_ref, gv_ref, wmix_ref, bias_ref, *refs):
    n_cast = (len(refs) - 2) // 2
    src_refs, (mix_ref, v_ref), dst_refs = refs[:n_cast], refs[n_cast:n_cast + 2], refs[n_cast + 2:]
    _ride_casts(src_refs, dst_refs, pl.program_id(0), GMLP_CAST_CHUNKS)
    u = _gelu(z_ref[:, :MIX_WIDTH])
    v = _rms_unit(_gelu(z_ref[:, MIX_WIDTH:2 * MIX_WIDTH])) * gv_ref[...]
    v_ref[...] = v
    vb = v.astype(BF16)
    bias = bias_ref[...]
    for g in range(N_GROUPS):
        lo, hi = g * HEAD_DIM, (g + 1) * HEAD_DIM
        mixed = jnp.dot(wmix_ref[g], vb[:, lo:hi], preferred_element_type=F32)
        mix_ref[:, lo:hi] = (u[:, lo:hi] * (mixed + bias[:, g:g + 1])).astype(BF16)


def _gmlp(z, g_v, wmix, bias, cast=()):
    n_prompt_tiles = M_PROMPT // CHUNK
    assert M_ALL // CHUNK >= GMLP_CAST_CHUNKS
    ride = _cast_plumbing(cast, GMLP_CAST_CHUNKS, lambda i: i)
    return pl.pallas_call(
        _gmlp_kernel,
        grid=(M_ALL // CHUNK,),
        in_specs=[
            pl.BlockSpec((CHUNK, z.shape[1]), lambda i: (i, 0)),
            pl.BlockSpec((1, MIX_WIDTH), lambda i: (0, 0)),
            pl.BlockSpec((None, N_GROUPS, CHUNK, CHUNK), lambda i: (i // n_prompt_tiles, 0, 0, 0)),
            pl.BlockSpec((None, CHUNK, N_GROUPS), lambda i: (i // n_prompt_tiles, 0, 0)),
        ] + ride.in_specs,
        out_specs=[
            pl.BlockSpec((CHUNK, MIX_WIDTH), lambda i: (i, 0)),
            pl.BlockSpec((CHUNK, MIX_WIDTH), lambda i: (jnp.maximum(i - n_prompt_tiles, 0), 0)),
        ] + ride.out_specs,
        out_shape=[
            jax.ShapeDtypeStruct((M_ALL, MIX_WIDTH), BF16),
            jax.ShapeDtypeStruct((M_SAMPLE, MIX_WIDTH), F32),
        ] + ride.out_shape,
        compiler_params=_params(("arbitrary",)),
        name="gmlp",
    )(z, g_v.reshape(1, MIX_WIDTH), wmix, bias, *ride.args)


def _mem_attend_one(q, k_ref, v_ref, o_ref, rows):
    for h in range(N_MEM_HEADS):
        cols = slice(h * HEAD_DIM, (h + 1) * HEAD_DIM)
        k = k_ref[pl.ds(h, N_MEM, stride=N_MEM_HEADS), :].astype(BF16)
        v = v_ref[pl.ds(h, N_MEM, stride=N_MEM_HEADS), :].astype(BF16)
        s = lax.dot_general(q[:, cols].astype(BF16), k, _NT, preferred_element_type=F32) * SCALE
        p = jnp.exp(s - jnp.max(s, axis=-1, keepdims=True))
        l = jnp.sum(p, axis=-1, keepdims=True)
        o = jnp.dot(p.astype(BF16), v, preferred_element_type=F32)
        o_ref[rows, cols] = (o / l).astype(o_ref.dtype)


def _mem_prompt_kernel(q_ref, k_ref, v_ref, o_ref):
    _mem_attend_one(q_ref[...], k_ref, v_ref, o_ref, slice(None))


def _mem_attend_prompt(z, q_col_block, mem_k, mem_v, layer, tq=512):
    nq = SEQ // tq
    kv_spec = pl.BlockSpec((None, None, N_MEM * N_MEM_HEADS, HEAD_DIM), lambda b, i: (layer, b, 0, 0))
    return pl.pallas_call(
        _mem_prompt_kernel,
        grid=(BATCH, nq),
        in_specs=[pl.BlockSpec((tq, MEM_WIDTH), lambda b, i: (b * nq + i, q_col_block)), kv_spec, kv_spec],
        out_specs=pl.BlockSpec((tq, MEM_WIDTH), lambda b, i: (b * nq + i, 0)),
        out_shape=jax.ShapeDtypeStruct((M_PROMPT, MEM_WIDTH), BF16),
        compiler_params=_params(("parallel", "parallel")),
        name="mem_attend_prompt",
    )(z, mem_k, mem_v)


def _mem_sample_kernel(q_ref, k_ref, v_ref, o_ref, *, n_seq):
    n_rows, n_cols = N_MEM_HEADS * T_PAD, N_MEM * N_MEM_HEADS
    same_head = (lax.broadcasted_iota(jnp.int32, (n_rows, n_cols), 1) % N_MEM_HEADS
                 == lax.broadcasted_iota(jnp.int32, (n_rows, n_cols), 0) // T_PAD)
    scores = []
    for s in range(n_seq):
        q8 = q_ref[s]
        q_all = jnp.concatenate([q8[:, h * HEAD_DIM:(h + 1) * HEAD_DIM] for h in range(N_MEM_HEADS)], axis=0)
        sc = lax.dot_general(q_all.astype(BF16), k_ref[s].astype(BF16), _NT,
                             preferred_element_type=F32) * SCALE
        scores.append(jnp.where(same_head, sc, -jnp.inf))
    probs = []
    for sc in scores:
        p = jnp.exp(sc - jnp.max(sc, axis=-1, keepdims=True))
        probs.append((p.astype(BF16), jnp.sum(p, axis=-1, keepdims=True)))
    for s, (p, l) in enumerate(probs):
        o = jnp.dot(p, v_ref[s].astype(BF16), preferred_element_type=F32) / l
        for h in range(N_MEM_HEADS):
            o_ref[s, :, h * HEAD_DIM:(h + 1) * HEAD_DIM] = o[h * T_PAD:(h + 1) * T_PAD]


def _mem_attend_sample(q8, mem_k, mem_v, layer, n_seq=8):
    kv_spec = pl.BlockSpec((None, n_seq, N_MEM * N_MEM_HEADS, HEAD_DIM), lambda i: (layer, i, 0, 0))
    tok_spec = pl.BlockSpec((n_seq, T_PAD, MEM_WIDTH), lambda i: (i, 0, 0))
    return pl.pallas_call(
        functools.partial(_mem_sample_kernel, n_seq=n_seq),
        grid=(DEC_BATCH // n_seq,),
        in_specs=[tok_spec, kv_spec, kv_spec],
        out_specs=tok_spec,
        out_shape=jax.ShapeDtypeStruct((DEC_BATCH, T_PAD, MEM_WIDTH), F32),
        compiler_params=_params(("parallel",)),
        name="mem_attend_sample",
    )(q8, mem_k, mem_v)


def _out_proj_kernel(*refs, split_x):
    if split_x:
        x_ref, xs_ref, mixp_ref, mixs_ref, mop_ref, mos_ref, w1_ref, w2_ref, o_ref = refs
    else:
        x_ref, mixp_ref, mixs_ref, mop_ref, mos_ref, w1_ref, w2_ref, o_ref = refs
        xs_ref = x_ref
    i = pl.program_id(0)

    def run(x_r, mix_r, mo_r):
        acc = jnp.dot(mix_r[...].astype(BF16), w1_ref[...], preferred_element_type=F32)
        acc += jnp.dot(mo_r[...].astype(BF16), w2_ref[...], preferred_element_type=F32)
        o_ref[...] = x_r[...] + acc

    pl.when(i < N_PROMPT_TILES)(lambda: run(x_ref, mixp_ref, mop_ref))
    pl.when(i >= N_PROMPT_TILES)(lambda: run(xs_ref, mixs_ref, mos_ref))


def _out_proj(x, mix, mo, w_out, tn=D_MODEL):
    split_x = isinstance(x, tuple)
    if split_x:
        in_specs, args = _split_specs(tn, lambda j: j), list(x)
    else:
        in_specs, args = [pl.BlockSpec((TM, tn), lambda i, j: (i, j))], [x]
    in_specs += _split_specs(MIX_WIDTH) + _split_specs(MEM_WIDTH)
    in_specs += [
        pl.BlockSpec((MIX_WIDTH, tn), lambda i, j: (0, j)),
        pl.BlockSpec((MEM_WIDTH, tn), lambda i, j: (MIX_WIDTH // MEM_WIDTH, j)),
    ]
    args += [mix[0], mix[1], mo[0], mo[1], w_out, w_out]
    return pl.pallas_call(
        functools.partial(_out_proj_kernel, split_x=split_x),
        grid=(M_ALL // TM, D_MODEL // tn),
        in_specs=in_specs,
        out_specs=pl.BlockSpec((TM, tn), lambda i, j: (i, j)),
        out_shape=jax.ShapeDtypeStruct((M_ALL, D_MODEL), F32),
        compiler_params=_params(("parallel", "parallel")),
        name="out_proj",
    )(*args)


def _mlp_kernel(*refs, final_norm, n_cast, cast_chunks):
    refs = list(refs)
    x_ref, g_ref, wu_ref, wd_ref = refs[:4]
    del refs[:4]
    if final_norm:
        gf_ref = refs.pop(0)
    src_refs = refs[:n_cast]
    del refs[:n_cast]
    if final_norm:
        yp_ref, ys_ref = refs[:2]
        del refs[:2]
    else:
        o_ref = refs.pop(0)
    dst_refs = refs[:n_cast]
    del refs[:n_cast]
    h_ref = refs.pop(0)
    acc_ref = refs.pop(0) if final_norm else o_ref
    i, f = pl.program_id(0), pl.program_id(1)
    last = f == pl.num_programs(1) - 1

    _ride_casts(src_refs, dst_refs, i * pl.num_programs(1) + f, cast_chunks)

    @pl.when(f == 0)
    def _():
        x = x_ref[...]
        h_ref[...] = (_rms_unit(x) * g_ref[...]).astype(BF16)
        acc_ref[...] = jnp.zeros_like(acc_ref) if final_norm else x

    a = jnp.maximum(jnp.dot(h_ref[...], wu_ref[...], preferred_element_type=F32), 0.0)
    acc_ref[...] += jnp.dot((a * a).astype(BF16), wd_ref[...], preferred_element_type=F32)

    if final_norm:
        def finish(ref):
            ref[...] = _rms_unit(x_ref[...] + acc_ref[...]) * gf_ref[...]

        pl.when(last & (i < N_PROMPT_TILES))(lambda: finish(yp_ref))
        pl.when(last & (i >= N_PROMPT_TILES))(lambda: finish(ys_ref))


def _mlp(x, g, w_up, w_down, g_final=None, cast=(), tf=1024):
    d, dff = w_up.shape
    nf = dff // tf
    final_norm = g_final is not None
    in_specs = [
        pl.BlockSpec((TM, d), lambda i, f: (i, 0)),
        pl.BlockSpec((1, d), lambda i, f: (0, 0)),
        pl.BlockSpec((d, tf), lambda i, f: (0, f)),
        pl.BlockSpec((tf, d), lambda i, f: (f, 0)),
    ]
    args = [x, g.reshape(1, d), w_up, w_down]
    if final_norm:
        in_specs.append(pl.BlockSpec((1, d), lambda i, f: (0, 0)))
        args.append(g_final.reshape(1, d))
        out_specs = [
            pl.BlockSpec((TM, d), lambda i, f: (jnp.minimum(i, N_PROMPT_TILES - 1), 0)),
            pl.BlockSpec((TM, d), lambda i, f: (0, 0)),
        ]
        out_shape = [jax.ShapeDtypeStruct((M_PROMPT, d), F32), jax.ShapeDtypeStruct((M_SAMPLE, d), F32)]
    else:
        out_specs = [pl.BlockSpec((TM, d), lambda i, f: (i, 0))]
        out_shape = [jax.ShapeDtypeStruct((M_ALL, d), F32)]

    cast_chunks = 128
    assert (M_ALL // TM) * nf >= cast_chunks
    ride = _cast_plumbing(cast, cast_chunks, lambda i, f: i * nf + f)
    scratch = [pltpu.VMEM((TM, d), BF16)] + ([pltpu.VMEM((TM, d), F32)] if final_norm else [])
    return pl.pallas_call(
        functools.partial(_mlp_kernel, final_norm=final_norm, n_cast=len(cast), cast_chunks=cast_chunks),
        grid=(M_ALL // TM, nf),
        in_specs=in_specs + ride.in_specs,
        out_specs=out_specs + ride.out_specs,
        out_shape=out_shape + ride.out_shape,
        scratch_shapes=scratch,
        compiler_params=_params(("arbitrary", "arbitrary")),
        name="mlp",
    )(*args, *ride.args)


def _moba_prompt_kernel(q_ref, k_ref, v_ref, o_ref, qa_ref, ka_ref, va_ref):
    lane = lax.broadcasted_iota(jnp.int32, (SEQ, HEAD_DIM), 1)
    row_blk = lax.broadcasted_iota(jnp.int32, (SEQ, HEAD_DIM), 0) // MOBA_BLOCK
    k = k_ref[...]
    q = q_ref[...]

    ka_ref[:, :HEAD_DIM] = k.astype(BF16)
    ka_ref[:, HEAD_DIM:] = (lane == row_blk).astype(BF16)
    va_ref[:, :HEAD_DIM] = v_ref[...].astype(BF16)
    va_ref[:, HEAD_DIM:] = (lane == 0).astype(BF16)

    kmean = jnp.mean(k.reshape(N_BLOCKS, MOBA_BLOCK, HEAD_DIM), axis=1)
    gate = _dot_nt_3pass(kmean, q)
    blk = lax.broadcasted_iota(jnp.int32, gate.shape, 0)
    q_blk = lax.broadcasted_iota(jnp.int32, gate.shape, 1) // MOBA_BLOCK
    sel = _top3_mask(gate, blk < q_blk, axis=0)
    eye = (lax.broadcasted_iota(jnp.int32, (N_BLOCKS, HEAD_DIM), 0)
           == lax.broadcasted_iota(jnp.int32, (N_BLOCKS, HEAD_DIM), 1)).astype(BF16)
    sel_cols = lax.dot_general(sel.astype(BF16), eye, _TN, preferred_element_type=F32)
    qa_ref[:, :HEAD_DIM] = (q * EXP2_SCALE).astype(BF16)
    qa_ref[:, HEAD_DIM:] = jnp.where(sel_cols > 0.5, 0.0, MASKED).astype(BF16)

    causal = (lax.broadcasted_iota(jnp.int32, (MOBA_BLOCK, MOBA_BLOCK), 1)
              <= lax.broadcasted_iota(jnp.int32, (MOBA_BLOCK, MOBA_BLOCK), 0))
    for qi in range(N_BLOCKS):
        rows = slice(qi * MOBA_BLOCK, (qi + 1) * MOBA_BLOCK)
        past = slice(0, qi * MOBA_BLOCK)
        s_own = lax.dot_general(qa_ref[rows, :HEAD_DIM], ka_ref[rows, :HEAD_DIM], _NT,
                                preferred_element_type=F32)
        s_own = jnp.where(causal, s_own, MASKED)
        m = jnp.max(s_own, axis=-1, keepdims=True)
        if qi:
            s_past = lax.dot_general(qa_ref[rows, :], ka_ref[past, :], _NT, preferred_element_type=F32)
            m = jnp.maximum(m, jnp.max(s_past, axis=-1, keepdims=True))
        pv = jnp.dot(jnp.exp2(s_own - m).astype(BF16), va_ref[rows, :], preferred_element_type=F32)
        if qi:
            pv += jnp.dot(jnp.exp2(s_past - m).astype(BF16), va_ref[past, :], preferred_element_type=F32)
        o_ref[rows, :] = (pv[:, :HEAD_DIM] / pv[:, HEAD_DIM:HEAD_DIM + 1]).astype(BF16)


def _moba_prompt(zq, kp, vp):
    kv_spec = pl.BlockSpec((None, None, SEQ, HEAD_DIM), lambda b, h: (b, h, 0, 0))
    return pl.pallas_call(
        _moba_prompt_kernel,
        grid=(BATCH, N_HEADS),
        in_specs=[pl.BlockSpec((SEQ, HEAD_DIM), lambda b, h: (b, h)), kv_spec, kv_spec],
        out_specs=pl.BlockSpec((SEQ, HEAD_DIM), lambda b, h: (b, h)),
        out_shape=jax.ShapeDtypeStruct((M_PROMPT, MIX_WIDTH), BF16),
        scratch_shapes=[pltpu.VMEM((SEQ, 2 * HEAD_DIM), BF16)] * 3,
        compiler_params=_params(("parallel", "parallel")),
        name="moba_prompt",
    )(zq, kp, vp)


def _head_diag(x):
    return jnp.concatenate(
        [x[h * T_PAD:(h + 1) * T_PAD, h * HEAD_DIM:(h + 1) * HEAD_DIM] for h in range(N_HEADS)], axis=1)


BLOCKS_PER_STEP = 4
PAGES_PER_STEP = BLOCKS_PER_STEP * PAGES_PER_BLOCK
STEPS_PER_SEQ = N_PAST_BLOCKS // BLOCKS_PER_STEP
assert N_PAST_BLOCKS % BLOCKS_PER_STEP == 0


def _moba_sample_kernel(pt_ref, q_ref, kn_ref, vn_ref, hm_ref, *refs):
    del pt_ref
    k_refs, v_refs = refs[:PAGES_PER_STEP], refs[PAGES_PER_STEP:2 * PAGES_PER_STEP]
    o_ref, qf_ref, qb_ref, kmean_ref, m_ref, l_ref, part_ref = refs[2 * PAGES_PER_STEP:]
    step = pl.program_id(1)
    part_row = lax.broadcasted_iota(jnp.int32, (N_PART, QCOLS), 0)

    @pl.when(step == 0)
    def _():
        q8 = q_ref[...]
        qrep = jnp.broadcast_to(q8[None], (QCOLS // T_PAD, T_PAD, MIX_WIDTH)).reshape(QCOLS, MIX_WIDTH)
        qbd = qrep * hm_ref[...]
        qf_ref[...] = qbd
        qb_ref[...] = qbd.astype(BF16)
        kmean_ref[...] = jnp.zeros_like(kmean_ref)
        m_ref[...] = jnp.zeros_like(m_ref)
        l_ref[...] = jnp.zeros_like(l_ref)

    qb = qb_ref[...]

    def scores_t(k_bf):
        return lax.dot_general(k_bf, qb, _NT, preferred_element_type=F32) * SCALE

    def load_block(page_refs, b, with_sum):
        rows, total = [], None
        for r in page_refs[b * PAGES_PER_BLOCK:(b + 1) * PAGES_PER_BLOCK]:
            slabs = [r[h] for h in range(N_HEADS)]
            rows.append(jnp.concatenate([x.astype(BF16) for x in slabs], axis=1))
            if with_sum:
                part = jnp.concatenate([jnp.sum(x, axis=0, keepdims=True) for x in slabs], axis=1)
                total = part if total is None else total + part
        return jnp.concatenate(rows, axis=0), total

    k_blocks = [load_block(k_refs, b, True) for b in range(BLOCKS_PER_STEP)]
    s_all = scores_t(jnp.concatenate([kb for kb, _ in k_blocks], axis=0))
    mean_row = lax.broadcasted_iota(jnp.int32, kmean_ref.shape, 0)
    m_new, l_new, kmean_new = m_ref[...], l_ref[...], kmean_ref[...]
    for b in range(BLOCKS_PER_STEP):
        n = step * BLOCKS_PER_STEP + b
        s = s_all[b * MOBA_BLOCK:(b + 1) * MOBA_BLOCK]
        mn = jnp.max(s, axis=0, keepdims=True)
        p = jnp.exp(s - mn)
        ln = jnp.sum(p, axis=0, keepdims=True)
        pv = jnp.dot(p.T.astype(BF16), load_block(v_refs, b, False)[0], preferred_element_type=F32)
        part_ref[n] = _head_diag(pv)
        m_new = jnp.where(part_row == n, mn, m_new)
        l_new = jnp.where(part_row == n, ln, l_new)
        kmean_new = jnp.where(mean_row == n, k_blocks[b][1] * (1.0 / MOBA_BLOCK), kmean_new)
    m_ref[...] = m_new
    l_ref[...] = l_new
    kmean_ref[...] = kmean_new

    @pl.when(step == STEPS_PER_SEQ - 1)
    def _():
        tail = jnp.zeros((128 - T_PAD, MIX_WIDTH), F32)
        knb = jnp.concatenate([kn_ref[...], tail], axis=0).astype(BF16)
        vnb = jnp.concatenate([vn_ref[...], tail], axis=0).astype(BF16)
        sc = scores_t(knb)
        key_t = lax.broadcasted_iota(jnp.int32, sc.shape, 0)
        qry_t = lax.broadcasted_iota(jnp.int32, sc.shape, 1) % T_PAD
        sc = jnp.where(key_t <= jnp.minimum(qry_t, DEC_SEQ - 1), sc, -jnp.inf)
        mc = jnp.max(sc, axis=0, keepdims=True)
        pc = jnp.exp(sc - mc)
        lc = jnp.sum(pc, axis=0, keepdims=True)
        oc = _head_diag(jnp.dot(pc.T.astype(BF16), vnb, preferred_element_type=F32))
        m_all = jnp.where(part_row == N_PAST_BLOCKS, mc, m_ref[...])
        l_all = jnp.where(part_row == N_PAST_BLOCKS, lc, l_ref[...])

        gate = _dot_nt_3pass(kmean_ref[...], qf_ref[...])
        sel = _top3_mask(gate, part_row < N_PAST_BLOCKS, axis=0)
        sel = jnp.where(part_row == N_PAST_BLOCKS, 1.0, sel)
        m_tot = jnp.max(jnp.where(sel > 0.5, m_all, -jnp.inf), axis=0, keepdims=True)
        w = jnp.where(sel > 0.5, jnp.exp(m_all - m_tot), 0.0)
        w = w * (1.0 / jnp.sum(w * l_all, axis=0, keepdims=True))
        w_cols = jnp.concatenate([w, jnp.zeros((QCOLS - N_PART, QCOLS), F32)], axis=0).T
        for h in range(N_HEADS):
            rows = slice(h * T_PAD, (h + 1) * T_PAD)
            cols = slice(h * HEAD_DIM, (h + 1) * HEAD_DIM)
            acc = w_cols[rows, N_PAST_BLOCKS:N_PAST_BLOCKS + 1] * oc[:, cols]
            for b in range(N_PAST_BLOCKS):
                acc += w_cols[rows, b:b + 1] * part_ref[b, :, cols]
            o_ref[:, cols] = acc


def _moba_sample(q8, kn8, vn8, cache_k, cache_v, pt_flat):
    head_of_col = jnp.arange(MIX_WIDTH, dtype=jnp.int32) // HEAD_DIM
    head_of_row = jnp.arange(QCOLS, dtype=jnp.int32) // T_PAD
    head_mask = (head_of_row[:, None] == head_of_col[None, :]).astype(F32)

    def tok_spec():
        return pl.BlockSpec((None, T_PAD, MIX_WIDTH), lambda s, n, pt: (s, 0, 0))

    def page_spec(j):
        return pl.BlockSpec((None, N_HEADS, PAGE_SIZE, HEAD_DIM),
                            lambda s, n, pt: (pt[s * N_PAGES + n * PAGES_PER_STEP + j], 0, 0, 0))

    page_specs = [page_spec(j) for j in range(PAGES_PER_STEP)]
    grid_spec = pltpu.PrefetchScalarGridSpec(
        num_scalar_prefetch=1,
        grid=(DEC_BATCH, STEPS_PER_SEQ),
        in_specs=[
            tok_spec(), tok_spec(), tok_spec(),
            pl.BlockSpec((QCOLS, MIX_WIDTH), lambda s, n, pt: (0, 0)),
        ] + page_specs + page_specs,
        out_specs=pl.BlockSpec((None, T_PAD, MIX_WIDTH), lambda s, n, pt: (s, 0, 0)),
        scratch_shapes=[
            pltpu.VMEM((QCOLS, MIX_WIDTH), F32),
            pltpu.VMEM((QCOLS, MIX_WIDTH), BF16),
            pltpu.VMEM((N_PART, MIX_WIDTH), F32),
            pltpu.VMEM((N_PART, QCOLS), F32),
            pltpu.VMEM((N_PART, QCOLS), F32),
            pltpu.VMEM((N_PAST_BLOCKS, T_PAD, MIX_WIDTH), F32),
        ],
    )
    return pl.pallas_call(
        _moba_sample_kernel,
        grid_spec=grid_spec,
        out_shape=jax.ShapeDtypeStruct((DEC_BATCH, T_PAD, MIX_WIDTH), F32),
        compiler_params=_params(("parallel", "arbitrary")),
        name="moba_sample",
    )(pt_flat, q8, kn8, vn8, head_mask, *([cache_k] * PAGES_PER_STEP), *([cache_v] * PAGES_PER_STEP))


def _rope_tables():
    half = ROT_DIM // 2
    inv = ROPE_THETA ** (-(np.arange(half, dtype=np.float64) * 2.0) / ROT_DIM)
    pos = np.concatenate([np.arange(SEQ), PAST_LEN + np.arange(TM) % DEC_SEQ]).astype(np.float64)
    ang = pos[:, None] * inv[None, :]
    cos, sin = np.cos(ang), np.sin(ang)
    zeros = np.zeros_like(cos)
    pad = np.zeros((pos.shape[0], HEAD_DIM - ROT_DIM))
    c = np.concatenate([cos, cos, pad + 1.0], axis=1)
    s1 = np.concatenate([-sin, zeros, pad], axis=1)
    s2 = np.concatenate([zeros, sin, pad], axis=1)
    return tuple(jnp.asarray(t, dtype=F32) for t in (c, s1, s2))


def _pad_tokens(rows):
    x = rows.reshape(DEC_BATCH, DEC_SEQ, rows.shape[-1])
    return jnp.pad(x, ((0, 0), (0, T_PAD - DEC_SEQ), (0, 0)))


def _unpad_tokens(x):
    return x[:, :DEC_SEQ].reshape(M_SAMPLE, x.shape[-1])


def kernel(x_prompt, x_sample, cache_k, cache_v, cache_mem_k, cache_mem_v, page_table, mem_prompt,
           g_mix, w_in_a, w_in_b, g_v, w_s, b_s, w_out, g_mlp, w_up, w_down, g_mem, w_mem_kv,
           g_kv, w_kv, g_final):
    depth = g_mix.shape[0]
    assert depth == 2 and w_in_a.shape[0] == 1 and w_in_b.shape[0] == 1

    x0 = (x_prompt.reshape(M_PROMPT, D_MODEL), x_sample.reshape(M_SAMPLE, D_MODEL))
    rope = _rope_tables()

    mem_rows = N_MEM * N_MEM_HEADS
    mem_k_p, mem_v_p = _mem_kv(mem_prompt.reshape(BATCH * N_MEM, D_MODEL), g_mem, w_mem_kv.astype(BF16))
    mem_k_s = cache_mem_k.reshape(depth, DEC_BATCH, mem_rows, HEAD_DIM)
    mem_v_s = cache_mem_v.reshape(depth, DEC_BATCH, mem_rows, HEAD_DIM)

    n_in_a = 2 * MIX_WIDTH + MEM_WIDTH
    z = _norm_matmul(x0, g_mix[0], w_in_a[0].astype(BF16), tn=n_in_a // 2)
    tril = jnp.tril(jnp.ones((CHUNK, CHUNK), bool))
    wmix_p = jnp.where(tril[None], w_s[0], 0.0)
    w_small = jnp.where(tril[None, :DEC_SEQ, :DEC_SEQ], w_s[0][:, :DEC_SEQ, :DEC_SEQ], 0.0)
    n_rep = CHUNK // DEC_SEQ
    wmix_s = jnp.einsum('ab,gts->gatbs', jnp.eye(n_rep, dtype=F32), w_small).reshape(N_GROUPS, CHUNK, CHUNK)
    wmix = jnp.stack([wmix_p, wmix_s]).astype(BF16)
    bias = jnp.stack([b_s[0].T, jnp.tile(b_s[0][:, :DEC_SEQ].T, (n_rep, 1))])
    mix, v_rows, w_down0, w_out0 = _gmlp(z, g_v[0], wmix, bias, cast=((w_down, 0), (w_out, 0)))
    q_blk = 2 * MIX_WIDTH // MEM_WIDTH
    mo = (_mem_attend_prompt(z, q_blk, mem_k_p, mem_v_p, 0),
          _unpad_tokens(_mem_attend_sample(_pad_tokens(z[M_PROMPT:, 2 * MIX_WIDTH:]), mem_k_s, mem_v_s, 0)))
    x = _out_proj(x0, (mix, mix[M_PROMPT:]), mo, w_out0)
    x, w_up1, w_down1, w_out1, w_kv_b, w_in_b1 = _mlp(
        x, g_mlp[0], w_up[0].astype(BF16), w_down0,
        cast=((w_up, 1), (w_down, 1), (w_out, 1), (w_kv, None), (w_in_b, 0)))

    kp, vp, kvs = _kv_proj(x, g_kv, w_kv_b, rope)
    zq = _norm_matmul(x, g_mix[1], w_in_b1, tn=D_MODEL, rope=rope, n_rope_cols=MIX_WIDTH)
    moba_p = _moba_prompt(zq, kp, vp)
    moba_s = _moba_sample(
        _pad_tokens(zq[M_PROMPT:, :MIX_WIDTH]),
        _pad_tokens(kvs[:, :MIX_WIDTH]),
        _pad_tokens(kvs[:, MIX_WIDTH:]),
        jnp.transpose(cache_k, (0, 2, 1, 3)),
        jnp.transpose(cache_v, (0, 2, 1, 3)),
        page_table.reshape(-1),
    )
    moba_s = _unpad_tokens(moba_s)
    q_blk = MIX_WIDTH // MEM_WIDTH
    mo = (_mem_attend_prompt(zq, q_blk, mem_k_p, mem_v_p, 1),
          _unpad_tokens(_mem_attend_sample(_pad_tokens(zq[M_PROMPT:, MIX_WIDTH:]), mem_k_s, mem_v_s, 1)))
    x = _out_proj(x, (moba_p, moba_s), mo, w_out1)
    y_p, y_s = _mlp(x, g_mlp[1], w_up1, w_down1, g_final=g_final)

    y_prompt = y_p.reshape(BATCH, SEQ, D_MODEL)
    y_sample = y_s.reshape(DEC_BATCH, DEC_SEQ, D_MODEL)
    k_prompt = jnp.transpose(kp, (0, 2, 1, 3))
    v_prompt = jnp.transpose(vp, (0, 2, 1, 3))
    k_sample = kvs[:, :MIX_WIDTH].reshape(DEC_BATCH, DEC_SEQ, N_HEADS, HEAD_DIM)
    v_sample = kvs[:, MIX_WIDTH:].reshape(DEC_BATCH, DEC_SEQ, N_HEADS, HEAD_DIM)
    mem_shape = (depth, BATCH, N_MEM, N_MEM_HEADS, HEAD_DIM)
    gmlp_v_sample = v_rows.reshape(1, DEC_BATCH, DEC_SEQ, MIX_WIDTH)
    return (y_prompt, y_sample, k_prompt, v_prompt, k_sample, v_sample,
            mem_k_p.reshape(mem_shape), mem_v_p.reshape(mem_shape), gmlp_v_sample)
```

```python
import functools
import math
from typing import NamedTuple

import jax
import jax.numpy as jnp
import numpy as np
from jax import lax
from jax.experimental import pallas as pl
from jax.experimental.pallas import tpu as pltpu

F32 = jnp.float32
BF16 = jnp.bfloat16

D_MODEL = 2048
BATCH = 4
SEQ = 2048
DEC_BATCH = 128
DEC_SEQ = 4
PAST_LEN = 2048
PAGE_SIZE = 128
HEAD_DIM = 128
N_MEM_HEADS = 4
MEM_WIDTH = N_MEM_HEADS * HEAD_DIM
MIX_WIDTH = D_MODEL - MEM_WIDTH
N_HEADS = MIX_WIDTH // HEAD_DIM
N_GROUPS = MIX_WIDTH // HEAD_DIM
CHUNK = 128
D_FF = 4 * D_MODEL
N_MEM = 256
MOBA_BLOCK = 256
MOBA_TOPK = 3
ROPE_THETA = 500000.0
ROT_DIM = HEAD_DIM // 4
EPS = 1e-6
SCALE = HEAD_DIM ** -0.5

M_PROMPT = BATCH * SEQ
M_SAMPLE = DEC_BATCH * DEC_SEQ
M_ALL = M_PROMPT + M_SAMPLE
TM = M_SAMPLE
N_PROMPT_TILES = M_PROMPT // TM
N_BLOCKS = SEQ // MOBA_BLOCK
N_PAST_BLOCKS = PAST_LEN // MOBA_BLOCK
PAGES_PER_BLOCK = MOBA_BLOCK // PAGE_SIZE
N_PAGES = PAST_LEN // PAGE_SIZE
T_PAD = 8
QCOLS = 128
N_PART = 16

assert PAGES_PER_BLOCK == 2 and PAST_LEN % MOBA_BLOCK == 0 and N_PAST_BLOCKS >= MOBA_TOPK
assert N_HEADS * T_PAD <= QCOLS and N_PAST_BLOCKS < N_PART

VMEM_LIMIT = 56 * 1024 * 1024

EXP2_SCALE = SCALE * math.log2(math.e)
MASKED = -1e30

_NT = (((1,), (1,)), ((), ()))
_TN = (((0,), (0,)), ((), ()))


def _params(sem):
    return pltpu.CompilerParams(dimension_semantics=sem, vmem_limit_bytes=VMEM_LIMIT)


def _rms_unit(x):
    return x * lax.rsqrt(jnp.mean(x * x, axis=-1, keepdims=True) + EPS)


def _gelu(x):
    c = math.sqrt(2.0 / math.pi)
    return x * (0.5 * (1.0 + jnp.tanh(c * (x + 0.044715 * (x * x * x)))))


def _rotate(seg, c, s1, s2):
    half = ROT_DIM // 2
    return (seg * c + pltpu.roll(seg, HEAD_DIM - half, axis=1) * s1
            + pltpu.roll(seg, half, axis=1) * s2)


def _dot_nt_3pass(a, b):
    def split(x):
        hi = x.astype(BF16)
        return hi, (x - hi.astype(F32)).astype(BF16)

    def nt(x, y):
        return lax.dot_general(x, y, _NT, preferred_element_type=F32)

    (ah, al), (bh, bl) = split(a), split(b)
    return nt(ah, bh) + (nt(ah, bl) + nt(al, bh))


def _top3_mask(gate, valid, axis):
    n = gate.shape[axis]
    idx = lax.broadcasted_iota(jnp.int32, gate.shape, axis)
    g = jnp.where(valid, gate, -jnp.inf)
    sel = jnp.zeros(gate.shape, F32)
    for _ in range(MOBA_TOPK):
        m = jnp.max(g, axis=axis, keepdims=True)
        first = jnp.min(jnp.where(g == m, idx, n), axis=axis, keepdims=True)
        pick = idx == first
        sel = jnp.where(pick, 1.0, sel)
        g = jnp.where(pick, -jnp.inf, g)
    return jnp.where(valid, sel, 0.0)


def _rope_spec():
    tiles_per_seq = SEQ // TM
    return pl.BlockSpec((TM, HEAD_DIM),
                        lambda i, j: (jnp.where(i < N_PROMPT_TILES, i % tiles_per_seq, tiles_per_seq), 0))


class _CastPlumbing(NamedTuple):
    in_specs: list
    out_specs: list
    out_shape: list
    args: list


def _cast_plumbing(cast, n_chunks, linear_step):
    def chunk(*ids):
        return jnp.minimum(linear_step(*ids), n_chunks - 1)

    plumbing = _CastPlumbing([], [], [], [])
    for w, layer in cast:
        rows, cols = w.shape[-2:]
        step = rows // n_chunks
        assert rows % n_chunks == 0 and step % 16 == 0
        if w.ndim == 3:
            plumbing.in_specs.append(
                pl.BlockSpec((None, step, cols), lambda *ids, layer=layer: (layer, chunk(*ids), 0)))
        else:
            plumbing.in_specs.append(pl.BlockSpec((step, cols), lambda *ids: (chunk(*ids), 0)))
        plumbing.out_specs.append(pl.BlockSpec((step, cols), lambda *ids: (chunk(*ids), 0)))
        plumbing.out_shape.append(jax.ShapeDtypeStruct((rows, cols), BF16))
        plumbing.args.append(w)
    return plumbing


def _ride_casts(src_refs, dst_refs, step, n_chunks):
    if not src_refs:
        return

    @pl.when(step < n_chunks)
    def _():
        for src, dst in zip(src_refs, dst_refs):
            dst[...] = src[...].astype(BF16)


def _split_specs(width, col_map=None):
    col = col_map or (lambda j: 0)
    return [
        pl.BlockSpec((TM, width), lambda i, j: (jnp.minimum(i, N_PROMPT_TILES - 1), col(j))),
        pl.BlockSpec((TM, width), lambda i, j: (0, col(j))),
    ]


def _norm_matmul_kernel(*refs, split_x, n_rope_heads):
    refs = list(refs)
    xs_ref = None
    x_ref = refs.pop(0)
    if split_x:
        xs_ref = refs.pop(0)
    g_ref, w_ref = refs.pop(0), refs.pop(0)
    if n_rope_heads:
        c_ref, s1_ref, s2_ref = refs.pop(0), refs.pop(0), refs.pop(0)
    o_ref, h_ref = refs
    i, j = pl.program_id(0), pl.program_id(1)

    def norm_from(ref):
        h_ref[...] = (_rms_unit(ref[...]) * g_ref[...]).astype(BF16)

    if split_x:
        pl.when((j == 0) & (i < N_PROMPT_TILES))(lambda: norm_from(x_ref))
        pl.when((j == 0) & (i >= N_PROMPT_TILES))(lambda: norm_from(xs_ref))
    else:
        pl.when(j == 0)(lambda: norm_from(x_ref))

    acc = jnp.dot(h_ref[...], w_ref[...], preferred_element_type=F32)

    if n_rope_heads:
        c, s1, s2 = c_ref[...], s1_ref[...], s2_ref[...]
        for hd in range(n_rope_heads):
            cols = slice(hd * HEAD_DIM, (hd + 1) * HEAD_DIM)
            o_ref[:, cols] = _rotate(acc[:, cols], c, s1, s2)
        o_ref[:, n_rope_heads * HEAD_DIM:] = acc[:, n_rope_heads * HEAD_DIM:]
    else:
        o_ref[...] = acc


def _norm_matmul(x, g, w, tn, rope=None, n_rope_cols=0):
    split_x = isinstance(x, tuple)
    k, n = w.shape
    assert n % tn == 0 and (n_rope_cols == 0 or tn == n)
    if split_x:
        in_specs, args = _split_specs(k), list(x)
    else:
        in_specs, args = [pl.BlockSpec((TM, k), lambda i, j: (i, 0))], [x]
    in_specs += [pl.BlockSpec((1, k), lambda i, j: (0, 0)), pl.BlockSpec((k, tn), lambda i, j: (0, j))]
    args += [g.reshape(1, k), w]
    if n_rope_cols:
        in_specs += [_rope_spec()] * 3
        args += list(rope)
    return pl.pallas_call(
        functools.partial(_norm_matmul_kernel, split_x=split_x, n_rope_heads=n_rope_cols // HEAD_DIM),
        grid=(M_ALL // TM, n // tn),
        in_specs=in_specs,
        out_specs=pl.BlockSpec((TM, tn), lambda i, j: (i, j)),
        out_shape=jax.ShapeDtypeStruct((M_ALL, n), F32),
        scratch_shapes=[pltpu.VMEM((TM, k), BF16)],
        compiler_params=_params(("parallel", "arbitrary")),
        name="norm_matmul",
    )(*args)


ROW_TILES_PER_SEQ = SEQ // TM


def _kv_proj_kernel(x_ref, g_ref, w_ref, c_ref, s1_ref, s2_ref, *out_refs, head_major):
    h = (_rms_unit(x_ref[...]) * g_ref[...]).astype(BF16)
    acc = jnp.dot(h, w_ref[...], preferred_element_type=F32)
    c, s1, s2 = c_ref[...], s1_ref[...], s2_ref[...]
    for hd in range(N_HEADS):
        k_cols = slice(hd * HEAD_DIM, (hd + 1) * HEAD_DIM)
        v_cols = slice(MIX_WIDTH + hd * HEAD_DIM, MIX_WIDTH + (hd + 1) * HEAD_DIM)
        k_seg = _rotate(acc[:, k_cols], c, s1, s2)
        if head_major:
            kp_ref, vp_ref = out_refs
            kp_ref[hd] = k_seg
            vp_ref[hd] = acc[:, v_cols]
        else:
            (kv_ref,) = out_refs
            kv_ref[:, k_cols] = k_seg
            kv_ref[:, v_cols] = acc[:, v_cols]


def _kv_proj(x, g, w, rope):
    k, n = w.shape
    common = [
        pl.BlockSpec((1, k), lambda i: (0, 0)),
        pl.BlockSpec((k, n), lambda i: (0, 0), pipeline_mode=pl.Buffered(1)),
    ]
    args = (g.reshape(1, k), w) + tuple(rope)
    head_block = pl.BlockSpec((None, N_HEADS, TM, HEAD_DIM),
                              lambda i: (i // ROW_TILES_PER_SEQ, 0, i % ROW_TILES_PER_SEQ, 0))
    head_shape = jax.ShapeDtypeStruct((BATCH, N_HEADS, SEQ, HEAD_DIM), F32)
    kp, vp = pl.pallas_call(
        functools.partial(_kv_proj_kernel, head_major=True),
        grid=(N_PROMPT_TILES,),
        in_specs=[pl.BlockSpec((TM, k), lambda i: (i, 0))] + common
        + [pl.BlockSpec((TM, HEAD_DIM), lambda i: (i % ROW_TILES_PER_SEQ, 0))] * 3,
        out_specs=[head_block, head_block],
        out_shape=[head_shape, head_shape],
        compiler_params=_params(("parallel",)),
        name="kv_proj_prompt",
    )(x, *args)
    kvs = pl.pallas_call(
        functools.partial(_kv_proj_kernel, head_major=False),
        grid=(1,),
        in_specs=[pl.BlockSpec((TM, k), lambda i: (N_PROMPT_TILES, 0))] + common
        + [pl.BlockSpec((TM, HEAD_DIM), lambda i: (ROW_TILES_PER_SEQ, 0))] * 3,
        out_specs=pl.BlockSpec((TM, n), lambda i: (0, 0)),
        out_shape=jax.ShapeDtypeStruct((M_SAMPLE, n), F32),
        compiler_params=_params(("arbitrary",)),
        name="kv_proj_sample",
    )(x, *args)
    return kp, vp, kvs


def _mem_kv_kernel(x_ref, g_ref, w_ref, k_ref, v_ref):
    h = (_rms_unit(x_ref[...]) * g_ref[...]).astype(BF16)
    acc = jnp.dot(h, w_ref[...], preferred_element_type=F32)
    rows = acc.shape[0]
    for hd in range(N_MEM_HEADS):
        k_ref[pl.ds(hd, rows, stride=N_MEM_HEADS), :] = acc[:, hd * HEAD_DIM:(hd + 1) * HEAD_DIM]
        v_ref[pl.ds(hd, rows, stride=N_MEM_HEADS), :] = acc[:, MEM_WIDTH + hd * HEAD_DIM:
                                                             MEM_WIDTH + (hd + 1) * HEAD_DIM]


def _mem_kv(mem_rows, g_mem, w_mem_kv):
    depth = g_mem.shape[0]
    out_block = pl.BlockSpec((None, None, N_MEM * N_MEM_HEADS, HEAD_DIM), lambda l, b: (l, b, 0, 0))
    out_shape = jax.ShapeDtypeStruct((depth, BATCH, N_MEM * N_MEM_HEADS, HEAD_DIM), F32)
    return pl.pallas_call(
        _mem_kv_kernel,
        grid=(depth, BATCH),
        in_specs=[
            pl.BlockSpec((N_MEM, D_MODEL), lambda l, b: (b, 0)),
            pl.BlockSpec((None, 1, D_MODEL), lambda l, b: (l, 0, 0)),
            pl.BlockSpec((None, D_MODEL, 2 * MEM_WIDTH), lambda l, b: (l, 0, 0)),
        ],
        out_specs=[out_block, out_block],
        out_shape=[out_shape, out_shape],
        compiler_params=_params(("parallel", "parallel")),
        name="mem_kv",
    )(mem_rows, g_mem.reshape(depth, 1, D_MODEL), w_mem_kv)


GMLP_CAST_CHUNKS = 64


def _gmlp_kernel(z_ref, gv_ref, wmix_ref, bias_ref, *refs):
    n_cast = (len(refs) - 2) // 2
    src_refs, (mix_ref, v_ref), dst_refs = refs[:n_cast], refs[n_cast:n_cast + 2], refs[n_cast + 2:]
    _ride_casts(src_refs, dst_refs, pl.program_id(0), GMLP_CAST_CHUNKS)
    u = _gelu(z_ref[:, :MIX_WIDTH])
    v = _rms_unit(_gelu(z_ref[:, MIX_WIDTH:2 * MIX_WIDTH])) * gv_ref[...]
    v_ref[...] = v
    vb = v.astype(BF16)
    bias = bias_ref[...]
    for g in range(N_GROUPS):
        lo, hi = g * HEAD_DIM, (g + 1) * HEAD_DIM
        mixed = jnp.dot(wmix_ref[g], vb[:, lo:hi], preferred_element_type=F32)
        mix_ref[:, lo:hi] = (u[:, lo:hi] * (mixed + bias[:, g:g + 1])).astype(BF16)


def _gmlp(z, g_v, wmix, bias, cast=()):
    n_prompt_tiles = M_PROMPT // CHUNK
    assert M_ALL // CHUNK >= GMLP_CAST_CHUNKS
    ride = _cast_plumbing(cast, GMLP_CAST_CHUNKS, lambda i: i)
    return pl.pallas_call(
        _gmlp_kernel,
        grid=(M_ALL // CHUNK,),
        in_specs=[
            pl.BlockSpec((CHUNK, z.shape[1]), lambda i: (i, 0)),
            pl.BlockSpec((1, MIX_WIDTH), lambda i: (0, 0)),
            pl.BlockSpec((None, N_GROUPS, CHUNK, CHUNK), lambda i: (i // n_prompt_tiles, 0, 0, 0)),
            pl.BlockSpec((None, CHUNK, N_GROUPS), lambda i: (i // n_prompt_tiles, 0, 0)),
        ] + ride.in_specs,
        out_specs=[
            pl.BlockSpec((CHUNK, MIX_WIDTH), lambda i: (i, 0)),
            pl.BlockSpec((CHUNK, MIX_WIDTH), lambda i: (jnp.maximum(i - n_prompt_tiles, 0), 0)),
        ] + ride.out_specs,
        out_shape=[
            jax.ShapeDtypeStruct((M_ALL, MIX_WIDTH), BF16),
            jax.ShapeDtypeStruct((M_SAMPLE, MIX_WIDTH), F32),
        ] + ride.out_shape,
        compiler_params=_params(("arbitrary",)),
        name="gmlp",
    )(z, g_v.reshape(1, MIX_WIDTH), wmix, bias, *ride.args)


def _mem_attend_one(q, k_ref, v_ref, o_ref, rows):
    for h in range(N_MEM_HEADS):
        cols = slice(h * HEAD_DIM, (h + 1) * HEAD_DIM)
        k = k_ref[pl.ds(h, N_MEM, stride=N_MEM_HEADS), :].astype(BF16)
        v = v_ref[pl.ds(h, N_MEM, stride=N_MEM_HEADS), :].astype(BF16)
        s = lax.dot_general(q[:, cols].astype(BF16), k, _NT, preferred_element_type=F32) * SCALE
        p = jnp.exp(s - jnp.max(s, axis=-1, keepdims=True))
        l = jnp.sum(p, axis=-1, keepdims=True)
        o = jnp.dot(p.astype(BF16), v, preferred_element_type=F32)
        o_ref[rows, cols] = (o / l).astype(o_ref.dtype)


def _mem_prompt_kernel(q_ref, k_ref, v_ref, o_ref):
    _mem_attend_one(q_ref[...], k_ref, v_ref, o_ref, slice(None))


def _mem_attend_prompt(z, q_col_block, mem_k, mem_v, layer, tq=512):
    nq = SEQ // tq
    kv_spec = pl.BlockSpec((None, None, N_MEM * N_MEM_HEADS, HEAD_DIM), lambda b, i: (layer, b, 0, 0))
    return pl.pallas_call(
        _mem_prompt_kernel,
        grid=(BATCH, nq),
        in_specs=[pl.BlockSpec((tq, MEM_WIDTH), lambda b, i: (b * nq + i, q_col_block)), kv_spec, kv_spec],
        out_specs=pl.BlockSpec((tq, MEM_WIDTH), lambda b, i: (b * nq + i, 0)),
        out_shape=jax.ShapeDtypeStruct((M_PROMPT, MEM_WIDTH), BF16),
        compiler_params=_params(("parallel", "parallel")),
        name="mem_attend_prompt",
    )(z, mem_k, mem_v)


def _mem_sample_kernel(q_ref, k_ref, v_ref, o_ref, *, n_seq):
    n_rows, n_cols = N_MEM_HEADS * T_PAD, N_MEM * N_MEM_HEADS
    same_head = (lax.broadcasted_iota(jnp.int32, (n_rows, n_cols), 1) % N_MEM_HEADS
                 == lax.broadcasted_iota(jnp.int32, (n_rows, n_cols), 0) // T_PAD)
    scores = []
    for s in range(n_seq):
        q8 = q_ref[s]
        q_all = jnp.concatenate([q8[:, h * HEAD_DIM:(h + 1) * HEAD_DIM] for h in range(N_MEM_HEADS)], axis=0)
        sc = lax.dot_general(q_all.astype(BF16), k_ref[s].astype(BF16), _NT,
                             preferred_element_type=F32) * SCALE
        scores.append(jnp.where(same_head, sc, -jnp.inf))
    probs = []
    for sc in scores:
        p = jnp.exp(sc - jnp.max(sc, axis=-1, keepdims=True))
        probs.append((p.astype(BF16), jnp.sum(p, axis=-1, keepdims=True)))
    for s, (p, l) in enumerate(probs):
        o = jnp.dot(p, v_ref[s].astype(BF16), preferred_element_type=F32) / l
        for h in range(N_MEM_HEADS):
            o_ref[s, :, h * HEAD_DIM:(h + 1) * HEAD_DIM] = o[h * T_PAD:(h + 1) * T_PAD]


def _mem_attend_sample(q8, mem_k, mem_v, layer, n_seq=8):
    kv_spec = pl.BlockSpec((None, n_seq, N_MEM * N_MEM_HEADS, HEAD_DIM), lambda i: (layer, i, 0, 0))
    tok_spec = pl.BlockSpec((n_seq, T_PAD, MEM_WIDTH), lambda i: (i, 0, 0))
    return pl.pallas_call(
        functools.partial(_mem_sample_kernel, n_seq=n_seq),
        grid=(DEC_BATCH // n_seq,),
        in_specs=[tok_spec, kv_spec, kv_spec],
        out_specs=tok_spec,
        out_shape=jax.ShapeDtypeStruct((DEC_BATCH, T_PAD, MEM_WIDTH), F32),
        compiler_params=_params(("parallel",)),
        name="mem_attend_sample",
    )(q8, mem_k, mem_v)


def _out_proj_kernel(*refs, split_x):
    if split_x:
        x_ref, xs_ref, mixp_ref, mixs_ref, mop_ref, mos_ref, w1_ref, w2_ref, o_ref = refs
    else:
        x_ref, mixp_ref, mixs_ref, mop_ref, mos_ref, w1_ref, w2_ref, o_ref = refs
        xs_ref = x_ref
    i = pl.program_id(0)

    def run(x_r, mix_r, mo_r):
        acc = jnp.dot(mix_r[...].astype(BF16), w1_ref[...], preferred_element_type=F32)
        acc += jnp.dot(mo_r[...].astype(BF16), w2_ref[...], preferred_element_type=F32)
        o_ref[...] = x_r[...] + acc

    pl.when(i < N_PROMPT_TILES)(lambda: run(x_ref, mixp_ref, mop_ref))
    pl.when(i >= N_PROMPT_TILES)(lambda: run(xs_ref, mixs_ref, mos_ref))


def _out_proj(x, mix, mo, w_out, tn=D_MODEL):
    split_x = isinstance(x, tuple)
    if split_x:
        in_specs, args = _split_specs(tn, lambda j: j), list(x)
    else:
        in_specs, args = [pl.BlockSpec((TM, tn), lambda i, j: (i, j))], [x]
    in_specs += _split_specs(MIX_WIDTH) + _split_specs(MEM_WIDTH)
    in_specs += [
        pl.BlockSpec((MIX_WIDTH, tn), lambda i, j: (0, j)),
        pl.BlockSpec((MEM_WIDTH, tn), lambda i, j: (MIX_WIDTH // MEM_WIDTH, j)),
    ]
    args += [mix[0], mix[1], mo[0], mo[1], w_out, w_out]
    return pl.pallas_call(
        functools.partial(_out_proj_kernel, split_x=split_x),
        grid=(M_ALL // TM, D_MODEL // tn),
        in_specs=in_specs,
        out_specs=pl.BlockSpec((TM, tn), lambda i, j: (i, j)),
        out_shape=jax.ShapeDtypeStruct((M_ALL, D_MODEL), F32),
        compiler_params=_params(("parallel", "parallel")),
        name="out_proj",
    )(*args)


def _mlp_kernel(*refs, final_norm, n_cast, cast_chunks):
    refs = list(refs)
    x_ref, g_ref, wu_ref, wd_ref = refs[:4]
    del refs[:4]
    if final_norm:
        gf_ref = refs.pop(0)
    src_refs = refs[:n_cast]
    del refs[:n_cast]
    if final_norm:
        yp_ref, ys_ref = refs[:2]
        del refs[:2]
    else:
        o_ref = refs.pop(0)
    dst_refs = refs[:n_cast]
    del refs[:n_cast]
    h_ref = refs.pop(0)
    acc_ref = refs.pop(0) if final_norm else o_ref
    i, f = pl.program_id(0), pl.program_id(1)
    last = f == pl.num_programs(1) - 1

    _ride_casts(src_refs, dst_refs, i * pl.num_programs(1) + f, cast_chunks)

    @pl.when(f == 0)
    def _():
        x = x_ref[...]
        h_ref[...] = (_rms_unit(x) * g_ref[...]).astype(BF16)
        acc_ref[...] = jnp.zeros_like(acc_ref) if final_norm else x

    a = jnp.maximum(jnp.dot(h_ref[...], wu_ref[...], preferred_element_type=F32), 0.0)
    acc_ref[...] += jnp.dot((a * a).astype(BF16), wd_ref[...], preferred_element_type=F32)

    if final_norm:
        def finish(ref):
            ref[...] = _rms_unit(x_ref[...] + acc_ref[...]) * gf_ref[...]

        pl.when(last & (i < N_PROMPT_TILES))(lambda: finish(yp_ref))
        pl.when(last & (i >= N_PROMPT_TILES))(lambda: finish(ys_ref))


def _mlp(x, g, w_up, w_down, g_final=None, cast=(), tf=1024):
    d, dff = w_up.shape
    nf = dff // tf
    final_norm = g_final is not None
    in_specs = [
        pl.BlockSpec((TM, d), lambda i, f: (i, 0)),
        pl.BlockSpec((1, d), lambda i, f: (0, 0)),
        pl.BlockSpec((d, tf), lambda i, f: (0, f)),
        pl.BlockSpec((tf, d), lambda i, f: (f, 0)),
    ]
    args = [x, g.reshape(1, d), w_up, w_down]
    if final_norm:
        in_specs.append(pl.BlockSpec((1, d), lambda i, f: (0, 0)))
        args.append(g_final.reshape(1, d))
        out_specs = [
            pl.BlockSpec((TM, d), lambda i, f: (jnp.minimum(i, N_PROMPT_TILES - 1), 0)),
            pl.BlockSpec((TM, d), lambda i, f: (0, 0)),
        ]
        out_shape = [jax.ShapeDtypeStruct((M_PROMPT, d), F32), jax.ShapeDtypeStruct((M_SAMPLE, d), F32)]
    else:
        out_specs = [pl.BlockSpec((TM, d), lambda i, f: (i, 0))]
        out_shape = [jax.ShapeDtypeStruct((M_ALL, d), F32)]

    cast_chunks = 128
    assert (M_ALL // TM) * nf >= cast_chunks
    ride = _cast_plumbing(cast, cast_chunks, lambda i, f: i * nf + f)
    scratch = [pltpu.VMEM((TM, d), BF16)] + ([pltpu.VMEM((TM, d), F32)] if final_norm else [])
    return pl.pallas_call(
        functools.partial(_mlp_kernel, final_norm=final_norm, n_cast=len(cast), cast_chunks=cast_chunks),
        grid=(M_ALL // TM, nf),
        in_specs=in_specs + ride.in_specs,
        out_specs=out_specs + ride.out_specs,
        out_shape=out_shape + ride.out_shape,
        scratch_shapes=scratch,
        compiler_params=_params(("arbitrary", "arbitrary")),
        name="mlp",
    )(*args, *ride.args)


def _prompt_setup(q_ref, k_ref, v_ref, qa_ref, ka_ref, va_ref):
    lane = lax.broadcasted_iota(jnp.int32, (SEQ, HEAD_DIM), 1)
    row_blk = lax.broadcasted_iota(jnp.int32, (SEQ, HEAD_DIM), 0) // MOBA_BLOCK
    k = k_ref[...]
    q = q_ref[...]

    ka_ref[:, :HEAD_DIM] = k.astype(BF16)
    ka_ref[:, HEAD_DIM:] = (lane == row_blk).astype(BF16)
    va_ref[:, :HEAD_DIM] = v_ref[...].astype(BF16)
    va_ref[:, HEAD_DIM:] = (lane == 0).astype(BF16)

    kmean = jnp.mean(k.reshape(N_BLOCKS, MOBA_BLOCK, HEAD_DIM), axis=1)
    gate = _dot_nt_3pass(kmean, q)
    blk = lax.broadcasted_iota(jnp.int32, gate.shape, 0)
    q_blk = lax.broadcasted_iota(jnp.int32, gate.shape, 1) // MOBA_BLOCK
    sel = _top3_mask(gate, blk < q_blk, axis=0)
    eye = (lax.broadcasted_iota(jnp.int32, (N_BLOCKS, HEAD_DIM), 0)
           == lax.broadcasted_iota(jnp.int32, (N_BLOCKS, HEAD_DIM), 1)).astype(BF16)
    sel_cols = lax.dot_general(sel.astype(BF16), eye, _TN, preferred_element_type=F32)
    qa_ref[:, :HEAD_DIM] = (q * EXP2_SCALE).astype(BF16)
    qa_ref[:, HEAD_DIM:] = jnp.where(sel_cols > 0.5, 0.0, MASKED).astype(BF16)


def _prompt_tile(qi, o_ref, qa_ref, ka_ref, va_ref):
    causal = (lax.broadcasted_iota(jnp.int32, (MOBA_BLOCK, MOBA_BLOCK), 1)
              <= lax.broadcasted_iota(jnp.int32, (MOBA_BLOCK, MOBA_BLOCK), 0))
    rows = slice(qi * MOBA_BLOCK, (qi + 1) * MOBA_BLOCK)
    past = slice(0, qi * MOBA_BLOCK)
    s_own = lax.dot_general(qa_ref[rows, :HEAD_DIM], ka_ref[rows, :HEAD_DIM], _NT, preferred_element_type=F32)
    s_own = jnp.where(causal, s_own, MASKED)
    m = jnp.max(s_own, axis=-1, keepdims=True)
    if qi:
        s_past = lax.dot_general(qa_ref[rows, :], ka_ref[past, :], _NT, preferred_element_type=F32)
        m = jnp.maximum(m, jnp.max(s_past, axis=-1, keepdims=True))
    pv = jnp.dot(jnp.exp2(s_own - m).astype(BF16), va_ref[rows, :], preferred_element_type=F32)
    if qi:
        pv += jnp.dot(jnp.exp2(s_past - m).astype(BF16), va_ref[past, :], preferred_element_type=F32)
    o_ref[rows, :] = (pv[:, :HEAD_DIM] / pv[:, HEAD_DIM:HEAD_DIM + 1]).astype(BF16)


PROMPT_SUBSTEPS = ((0,), (1, 2, 3), (4, 5), (6,), (7,))
N_PROMPT_UNITS = BATCH * N_HEADS
assert sorted(qi for tiles in PROMPT_SUBSTEPS for qi in tiles) == list(range(N_BLOCKS))


def _head_diag(x):
    return jnp.concatenate(
        [x[h * T_PAD:(h + 1) * T_PAD, h * HEAD_DIM:(h + 1) * HEAD_DIM] for h in range(N_HEADS)], axis=1)


BLOCKS_PER_STEP = 4
PAGES_PER_STEP = BLOCKS_PER_STEP * PAGES_PER_BLOCK
STEPS_PER_SEQ = N_PAST_BLOCKS // BLOCKS_PER_STEP
assert N_PAST_BLOCKS % BLOCKS_PER_STEP == 0


def _moba_kernel(pt_ref, q_ref, kn_ref, vn_ref, hm_ref, *refs):
    del pt_ref
    k_refs, v_refs = refs[:PAGES_PER_STEP], refs[PAGES_PER_STEP:2 * PAGES_PER_STEP]
    (pq_ref, pk_ref, pv_ref, o_ref, po_ref,
     qf_ref, qb_ref, kmean_ref, m_ref, l_ref, part_ref, qa_ref, ka_ref, va_ref) = refs[2 * PAGES_PER_STEP:]
    step = pl.program_id(1)
    part_row = lax.broadcasted_iota(jnp.int32, (N_PART, QCOLS), 0)

    t = pl.program_id(0) * STEPS_PER_SEQ + step
    prompt_active = t < N_PROMPT_UNITS * len(PROMPT_SUBSTEPS)
    for sub, tiles in enumerate(PROMPT_SUBSTEPS):
        @pl.when(prompt_active & (t % len(PROMPT_SUBSTEPS) == sub))
        def _(sub=sub, tiles=tiles):
            if sub == 0:
                _prompt_setup(pq_ref, pk_ref, pv_ref, qa_ref, ka_ref, va_ref)
            for qi in tiles:
                _prompt_tile(qi, po_ref, qa_ref, ka_ref, va_ref)

    @pl.when(step == 0)
    def _():
        q8 = q_ref[...]
        qrep = jnp.broadcast_to(q8[None], (QCOLS // T_PAD, T_PAD, MIX_WIDTH)).reshape(QCOLS, MIX_WIDTH)
        qbd = qrep * hm_ref[...]
        qf_ref[...] = qbd
        qb_ref[...] = qbd.astype(BF16)
        kmean_ref[...] = jnp.zeros_like(kmean_ref)
        m_ref[...] = jnp.zeros_like(m_ref)
        l_ref[...] = jnp.zeros_like(l_ref)

    qb = qb_ref[...]

    def scores_t(k_bf):
        return lax.dot_general(k_bf, qb, _NT, preferred_element_type=F32) * SCALE

    def load_block(page_refs, b, with_sum):
        rows, total = [], None
        for r in page_refs[b * PAGES_PER_BLOCK:(b + 1) * PAGES_PER_BLOCK]:
            slabs = [r[h] for h in range(N_HEADS)]
            rows.append(jnp.concatenate([x.astype(BF16) for x in slabs], axis=1))
            if with_sum:
                part = jnp.concatenate([jnp.sum(x, axis=0, keepdims=True) for x in slabs], axis=1)
                total = part if total is None else total + part
        return jnp.concatenate(rows, axis=0), total

    k_blocks = [load_block(k_refs, b, True) for b in range(BLOCKS_PER_STEP)]
    s_all = scores_t(jnp.concatenate([kb for kb, _ in k_blocks], axis=0))
    mean_row = lax.broadcasted_iota(jnp.int32, kmean_ref.shape, 0)
    m_new, l_new, kmean_new = m_ref[...], l_ref[...], kmean_ref[...]
    for b in range(BLOCKS_PER_STEP):
        n = step * BLOCKS_PER_STEP + b
        s = s_all[b * MOBA_BLOCK:(b + 1) * MOBA_BLOCK]
        mn = jnp.max(s, axis=0, keepdims=True)
        p = jnp.exp(s - mn)
        ln = jnp.sum(p, axis=0, keepdims=True)
        pv = jnp.dot(p.T.astype(BF16), load_block(v_refs, b, False)[0], preferred_element_type=F32)
        part_ref[n] = _head_diag(pv)
        m_new = jnp.where(part_row == n, mn, m_new)
        l_new = jnp.where(part_row == n, ln, l_new)
        kmean_new = jnp.where(mean_row == n, k_blocks[b][1] * (1.0 / MOBA_BLOCK), kmean_new)
    m_ref[...] = m_new
    l_ref[...] = l_new
    kmean_ref[...] = kmean_new

    @pl.when(step == STEPS_PER_SEQ - 1)
    def _():
        tail = jnp.zeros((128 - T_PAD, MIX_WIDTH), F32)
        knb = jnp.concatenate([kn_ref[...], tail], axis=0).astype(BF16)
        vnb = jnp.concatenate([vn_ref[...], tail], axis=0).astype(BF16)
        sc = scores_t(knb)
        key_t = lax.broadcasted_iota(jnp.int32, sc.shape, 0)
        qry_t = lax.broadcasted_iota(jnp.int32, sc.shape, 1) % T_PAD
        sc = jnp.where(key_t <= jnp.minimum(qry_t, DEC_SEQ - 1), sc, -jnp.inf)
        mc = jnp.max(sc, axis=0, keepdims=True)
        pc = jnp.exp(sc - mc)
        lc = jnp.sum(pc, axis=0, keepdims=True)
        oc = _head_diag(jnp.dot(pc.T.astype(BF16), vnb, preferred_element_type=F32))
        m_all = jnp.where(part_row == N_PAST_BLOCKS, mc, m_ref[...])
        l_all = jnp.where(part_row == N_PAST_BLOCKS, lc, l_ref[...])

        gate = _dot_nt_3pass(kmean_ref[...], qf_ref[...])
        sel = _top3_mask(gate, part_row < N_PAST_BLOCKS, axis=0)
        sel = jnp.where(part_row == N_PAST_BLOCKS, 1.0, sel)
        m_tot = jnp.max(jnp.where(sel > 0.5, m_all, -jnp.inf), axis=0, keepdims=True)
        w = jnp.where(sel > 0.5, jnp.exp(m_all - m_tot), 0.0)
        w = w * (1.0 / jnp.sum(w * l_all, axis=0, keepdims=True))
        w_cols = jnp.concatenate([w, jnp.zeros((QCOLS - N_PART, QCOLS), F32)], axis=0).T
        for h in range(N_HEADS):
            rows = slice(h * T_PAD, (h + 1) * T_PAD)
            cols = slice(h * HEAD_DIM, (h + 1) * HEAD_DIM)
            acc = w_cols[rows, N_PAST_BLOCKS:N_PAST_BLOCKS + 1] * oc[:, cols]
            for b in range(N_PAST_BLOCKS):
                acc += w_cols[rows, b:b + 1] * part_ref[b, :, cols]
            o_ref[:, cols] = acc


def _moba(zq, kp, vp, q8, kn8, vn8, cache_k, cache_v, pt_flat):
    n_sub = len(PROMPT_SUBSTEPS)
    assert DEC_BATCH * STEPS_PER_SEQ >= N_PROMPT_UNITS * n_sub

    def unit(s, n):
        return jnp.minimum((s * STEPS_PER_SEQ + n) // n_sub, N_PROMPT_UNITS - 1)

    prompt_rows = pl.BlockSpec((SEQ, HEAD_DIM), lambda s, n, pt: (unit(s, n) // N_HEADS, unit(s, n) % N_HEADS))
    prompt_kv = pl.BlockSpec((None, None, SEQ, HEAD_DIM),
                             lambda s, n, pt: (unit(s, n) // N_HEADS, unit(s, n) % N_HEADS, 0, 0))
    head_of_col = jnp.arange(MIX_WIDTH, dtype=jnp.int32) // HEAD_DIM
    head_of_row = jnp.arange(QCOLS, dtype=jnp.int32) // T_PAD
    head_mask = (head_of_row[:, None] == head_of_col[None, :]).astype(F32)

    def tok_spec():
        return pl.BlockSpec((None, T_PAD, MIX_WIDTH), lambda s, n, pt: (s, 0, 0))

    def page_spec(j):
        return pl.BlockSpec((None, N_HEADS, PAGE_SIZE, HEAD_DIM),
                            lambda s, n, pt: (pt[s * N_PAGES + n * PAGES_PER_STEP + j], 0, 0, 0))

    page_specs = [page_spec(j) for j in range(PAGES_PER_STEP)]
    grid_spec = pltpu.PrefetchScalarGridSpec(
        num_scalar_prefetch=1,
        grid=(DEC_BATCH, STEPS_PER_SEQ),
        in_specs=[
            tok_spec(), tok_spec(), tok_spec(),
            pl.BlockSpec((QCOLS, MIX_WIDTH), lambda s, n, pt: (0, 0)),
        ] + page_specs + page_specs + [prompt_rows, prompt_kv, prompt_kv],
        out_specs=[pl.BlockSpec((None, T_PAD, MIX_WIDTH), lambda s, n, pt: (s, 0, 0)), prompt_rows],
        scratch_shapes=[
            pltpu.VMEM((QCOLS, MIX_WIDTH), F32),
            pltpu.VMEM((QCOLS, MIX_WIDTH), BF16),
            pltpu.VMEM((N_PART, MIX_WIDTH), F32),
            pltpu.VMEM((N_PART, QCOLS), F32),
            pltpu.VMEM((N_PART, QCOLS), F32),
            pltpu.VMEM((N_PAST_BLOCKS, T_PAD, MIX_WIDTH), F32),
        ] + [pltpu.VMEM((SEQ, 2 * HEAD_DIM), BF16)] * 3,
    )
    moba_s, moba_p = pl.pallas_call(
        _moba_kernel,
        grid_spec=grid_spec,
        out_shape=[jax.ShapeDtypeStruct((DEC_BATCH, T_PAD, MIX_WIDTH), F32),
                   jax.ShapeDtypeStruct((M_PROMPT, MIX_WIDTH), BF16)],
        compiler_params=_params(("arbitrary", "arbitrary")),
        name="moba",
    )(pt_flat, q8, kn8, vn8, head_mask, *([cache_k] * PAGES_PER_STEP), *([cache_v] * PAGES_PER_STEP),
      zq, kp, vp)
    return moba_p, moba_s


def _rope_tables():
    half = ROT_DIM // 2
    inv = ROPE_THETA ** (-(np.arange(half, dtype=np.float64) * 2.0) / ROT_DIM)
    pos = np.concatenate([np.arange(SEQ), PAST_LEN + np.arange(TM) % DEC_SEQ]).astype(np.float64)
    ang = pos[:, None] * inv[None, :]
    cos, sin = np.cos(ang), np.sin(ang)
    zeros = np.zeros_like(cos)
    pad = np.zeros((pos.shape[0], HEAD_DIM - ROT_DIM))
    c = np.concatenate([cos, cos, pad + 1.0], axis=1)
    s1 = np.concatenate([-sin, zeros, pad], axis=1)
    s2 = np.concatenate([zeros, sin, pad], axis=1)
    return tuple(jnp.asarray(t, dtype=F32) for t in (c, s1, s2))


def _pad_tokens(rows):
    x = rows.reshape(DEC_BATCH, DEC_SEQ, rows.shape[-1])
    return jnp.pad(x, ((0, 0), (0, T_PAD - DEC_SEQ), (0, 0)))


def _unpad_tokens(x):
    return x[:, :DEC_SEQ].reshape(M_SAMPLE, x.shape[-1])


def kernel(x_prompt, x_sample, cache_k, cache_v, cache_mem_k, cache_mem_v, page_table, mem_prompt,
           g_mix, w_in_a, w_in_b, g_v, w_s, b_s, w_out, g_mlp, w_up, w_down, g_mem, w_mem_kv,
           g_kv, w_kv, g_final):
    depth = g_mix.shape[0]
    assert depth == 2 and w_in_a.shape[0] == 1 and w_in_b.shape[0] == 1

    x0 = (x_prompt.reshape(M_PROMPT, D_MODEL), x_sample.reshape(M_SAMPLE, D_MODEL))
    rope = _rope_tables()

    mem_rows = N_MEM * N_MEM_HEADS
    mem_k_p, mem_v_p = _mem_kv(mem_prompt.reshape(BATCH * N_MEM, D_MODEL), g_mem, w_mem_kv.astype(BF16))
    mem_k_s = cache_mem_k.reshape(depth, DEC_BATCH, mem_rows, HEAD_DIM)
    mem_v_s = cache_mem_v.reshape(depth, DEC_BATCH, mem_rows, HEAD_DIM)

    n_in_a = 2 * MIX_WIDTH + MEM_WIDTH
    z = _norm_matmul(x0, g_mix[0], w_in_a[0].astype(BF16), tn=n_in_a // 2)
    tril = jnp.tril(jnp.ones((CHUNK, CHUNK), bool))
    wmix_p = jnp.where(tril[None], w_s[0], 0.0)
    w_small = jnp.where(tril[None, :DEC_SEQ, :DEC_SEQ], w_s[0][:, :DEC_SEQ, :DEC_SEQ], 0.0)
    n_rep = CHUNK // DEC_SEQ
    wmix_s = jnp.einsum('ab,gts->gatbs', jnp.eye(n_rep, dtype=F32), w_small).reshape(N_GROUPS, CHUNK, CHUNK)
    wmix = jnp.stack([wmix_p, wmix_s]).astype(BF16)
    bias = jnp.stack([b_s[0].T, jnp.tile(b_s[0][:, :DEC_SEQ].T, (n_rep, 1))])
    mix, v_rows, w_down0, w_out0 = _gmlp(z, g_v[0], wmix, bias, cast=((w_down, 0), (w_out, 0)))
    q_blk = 2 * MIX_WIDTH // MEM_WIDTH
    mo = (_mem_attend_prompt(z, q_blk, mem_k_p, mem_v_p, 0),
          _unpad_tokens(_mem_attend_sample(_pad_tokens(z[M_PROMPT:, 2 * MIX_WIDTH:]), mem_k_s, mem_v_s, 0)))
    x = _out_proj(x0, (mix, mix[M_PROMPT:]), mo, w_out0)
    x, w_up1, w_down1, w_out1, w_kv_b, w_in_b1 = _mlp(
        x, g_mlp[0], w_up[0].astype(BF16), w_down0,
        cast=((w_up, 1), (w_down, 1), (w_out, 1), (w_kv, None), (w_in_b, 0)))

    kp, vp, kvs = _kv_proj(x, g_kv, w_kv_b, rope)
    zq = _norm_matmul(x, g_mix[1], w_in_b1, tn=D_MODEL, rope=rope, n_rope_cols=MIX_WIDTH)
    moba_p, moba_s = _moba(
        zq, kp, vp,
        _pad_tokens(zq[M_PROMPT:, :MIX_WIDTH]),
        _pad_tokens(kvs[:, :MIX_WIDTH]),
        _pad_tokens(kvs[:, MIX_WIDTH:]),
        jnp.transpose(cache_k, (0, 2, 1, 3)),
        jnp.transpose(cache_v, (0, 2, 1, 3)),
        page_table.reshape(-1),
    )
    moba_s = _unpad_tokens(moba_s)
    q_blk = MIX_WIDTH // MEM_WIDTH
    mo = (_mem_attend_prompt(zq, q_blk, mem_k_p, mem_v_p, 1),
          _unpad_tokens(_mem_attend_sample(_pad_tokens(zq[M_PROMPT:, MIX_WIDTH:]), mem_k_s, mem_v_s, 1)))
    x = _out_proj(x, (moba_p, moba_s), mo, w_out1)
    y_p, y_s = _mlp(x, g_mlp[1], w_up1, w_down1, g_final=g_final)

    y_prompt = y_p.reshape(BATCH, SEQ, D_MODEL)
    y_sample = y_s.reshape(DEC_BATCH, DEC_SEQ, D_MODEL)
    k_prompt = jnp.transpose(kp, (0, 2, 1, 3))
    v_prompt = jnp.transpose(vp, (0, 2, 1, 3))
    k_sample = kvs[:, :MIX_WIDTH].reshape(DEC_BATCH, DEC_SEQ, N_HEADS, HEAD_DIM)
    v_sample = kvs[:, MIX_WIDTH:].reshape(DEC_BATCH, DEC_SEQ, N_HEADS, HEAD_DIM)
    mem_shape = (depth, BATCH, N_MEM, N_MEM_HEADS, HEAD_DIM)
    gmlp_v_sample = v_rows.reshape(1, DEC_BATCH, DEC_SEQ, MIX_WIDTH)
    return (y_prompt, y_sample, k_prompt, v_prompt, k_sample, v_sample,
            mem_k_p.reshape(mem_shape), mem_v_p.reshape(mem_shape), gmlp_v_sample)
```

```python
import functools
import math
from typing import NamedTuple

import jax
import jax.numpy as jnp
import numpy as np
from jax import lax
from jax.experimental import pallas as pl
from jax.experimental.pallas import tpu as pltpu

F32 = jnp.float32
BF16 = jnp.bfloat16

D_MODEL = 2048
BATCH = 4
SEQ = 2048
DEC_BATCH = 128
DEC_SEQ = 4
PAST_LEN = 2048
PAGE_SIZE = 128
HEAD_DIM = 128
N_MEM_HEADS = 4
MEM_WIDTH = N_MEM_HEADS * HEAD_DIM
MIX_WIDTH = D_MODEL - MEM_WIDTH
N_HEADS = MIX_WIDTH // HEAD_DIM
N_GROUPS = MIX_WIDTH // HEAD_DIM
CHUNK = 128
D_FF = 4 * D_MODEL
N_MEM = 256
MOBA_BLOCK = 256
MOBA_TOPK = 3
ROPE_THETA = 500000.0
ROT_DIM = HEAD_DIM // 4
EPS = 1e-6
SCALE = HEAD_DIM ** -0.5

M_PROMPT = BATCH * SEQ
M_SAMPLE = DEC_BATCH * DEC_SEQ
M_ALL = M_PROMPT + M_SAMPLE
TM = M_SAMPLE
N_PROMPT_TILES = M_PROMPT // TM
N_BLOCKS = SEQ // MOBA_BLOCK
N_PAST_BLOCKS = PAST_LEN // MOBA_BLOCK
PAGES_PER_BLOCK = MOBA_BLOCK // PAGE_SIZE
N_PAGES = PAST_LEN // PAGE_SIZE
T_PAD = 8
QCOLS = 128
N_PART = 16

assert PAGES_PER_BLOCK == 2 and PAST_LEN % MOBA_BLOCK == 0 and N_PAST_BLOCKS >= MOBA_TOPK
assert N_HEADS * T_PAD <= QCOLS and N_PAST_BLOCKS < N_PART

VMEM_LIMIT = 56 * 1024 * 1024

EXP2_SCALE = SCALE * math.log2(math.e)
MASKED = -1e30

_NT = (((1,), (1,)), ((), ()))
_TN = (((0,), (0,)), ((), ()))


def _params(sem):
    return pltpu.CompilerParams(dimension_semantics=sem, vmem_limit_bytes=VMEM_LIMIT)


def _rms_unit(x):
    return x * lax.rsqrt(jnp.mean(x * x, axis=-1, keepdims=True) + EPS)


def _gelu(x):
    c = math.sqrt(2.0 / math.pi)
    return x * (0.5 * (1.0 + jnp.tanh(c * (x + 0.044715 * (x * x * x)))))


def _rotate(seg, c, s1, s2):
    half = ROT_DIM // 2
    return (seg * c + pltpu.roll(seg, HEAD_DIM - half, axis=1) * s1
            + pltpu.roll(seg, half, axis=1) * s2)


def _dot_nt_3pass(a, b):
    def split(x):
        hi = x.astype(BF16)
        return hi, (x - hi.astype(F32)).astype(BF16)

    def nt(x, y):
        return lax.dot_general(x, y, _NT, preferred_element_type=F32)

    (ah, al), (bh, bl) = split(a), split(b)
    return nt(ah, bh) + (nt(ah, bl) + nt(al, bh))


def _top3_mask(gate, valid, axis):
    n = gate.shape[axis]
    idx = lax.broadcasted_iota(jnp.int32, gate.shape, axis)
    g = jnp.where(valid, gate, -jnp.inf)
    sel = jnp.zeros(gate.shape, F32)
    for _ in range(MOBA_TOPK):
        m = jnp.max(g, axis=axis, keepdims=True)
        first = jnp.min(jnp.where(g == m, idx, n), axis=axis, keepdims=True)
        pick = idx == first
        sel = jnp.where(pick, 1.0, sel)
        g = jnp.where(pick, -jnp.inf, g)
    return jnp.where(valid, sel, 0.0)


def _rope_spec():
    tiles_per_seq = SEQ // TM
    return pl.BlockSpec((TM, HEAD_DIM),
                        lambda i, j: (jnp.where(i < N_PROMPT_TILES, i % tiles_per_seq, tiles_per_seq), 0))


class _CastPlumbing(NamedTuple):
    in_specs: list
    out_specs: list
    out_shape: list
    args: list


def _cast_plumbing(cast, n_chunks, linear_step):
    def chunk(*ids):
        return jnp.minimum(linear_step(*ids), n_chunks - 1)

    plumbing = _CastPlumbing([], [], [], [])
    for w, layer in cast:
        rows, cols = w.shape[-2:]
        step = rows // n_chunks
        assert rows % n_chunks == 0 and step % 16 == 0
        if w.ndim == 3:
            plumbing.in_specs.append(
                pl.BlockSpec((None, step, cols), lambda *ids, layer=layer: (layer, chunk(*ids), 0)))
        else:
            plumbing.in_specs.append(pl.BlockSpec((step, cols), lambda *ids: (chunk(*ids), 0)))
        plumbing.out_specs.append(pl.BlockSpec((step, cols), lambda *ids: (chunk(*ids), 0)))
        plumbing.out_shape.append(jax.ShapeDtypeStruct((rows, cols), BF16))
        plumbing.args.append(w)
    return plumbing


def _ride_casts(src_refs, dst_refs, step, n_chunks):
    if not src_refs:
        return

    @pl.when(step < n_chunks)
    def _():
        for src, dst in zip(src_refs, dst_refs):
            dst[...] = src[...].astype(BF16)


def _split_specs(width, col_map=None):
    col = col_map or (lambda j: 0)
    return [
        pl.BlockSpec((TM, width), lambda i, j: (jnp.minimum(i, N_PROMPT_TILES - 1), col(j))),
        pl.BlockSpec((TM, width), lambda i, j: (0, col(j))),
    ]


def _norm_matmul_kernel(*refs, split_x, n_rope_heads, rows_inner, n_cast, cast_chunks):
    refs = list(refs)
    xs_ref = None
    x_ref = refs.pop(0)
    if split_x:
        xs_ref = refs.pop(0)
    g_ref, w_ref = refs.pop(0), refs.pop(0)
    if n_rope_heads:
        c_ref, s1_ref, s2_ref = refs.pop(0), refs.pop(0), refs.pop(0)
    src_refs = refs[:n_cast]
    del refs[:n_cast]
    o_ref = refs.pop(0)
    dst_refs = refs[:n_cast]
    (h_ref,) = refs[n_cast:]
    if rows_inner:
        j, i = pl.program_id(0), pl.program_id(1)
        new_rows = True
    else:
        i, j = pl.program_id(0), pl.program_id(1)
        new_rows = j == 0
    _ride_casts(src_refs, dst_refs, pl.program_id(0) * pl.num_programs(1) + pl.program_id(1), cast_chunks)

    def norm_from(ref):
        h_ref[...] = (_rms_unit(ref[...]) * g_ref[...]).astype(BF16)

    if split_x:
        pl.when(new_rows & (i < N_PROMPT_TILES))(lambda: norm_from(x_ref))
        pl.when(new_rows & (i >= N_PROMPT_TILES))(lambda: norm_from(xs_ref))
    elif rows_inner:
        norm_from(x_ref)
    else:
        pl.when(new_rows)(lambda: norm_from(x_ref))

    acc = jnp.dot(h_ref[...], w_ref[...], preferred_element_type=F32)

    if n_rope_heads:
        c, s1, s2 = c_ref[...], s1_ref[...], s2_ref[...]
        for hd in range(n_rope_heads):
            cols = slice(hd * HEAD_DIM, (hd + 1) * HEAD_DIM)
            o_ref[:, cols] = _rotate(acc[:, cols], c, s1, s2)
        o_ref[:, n_rope_heads * HEAD_DIM:] = acc[:, n_rope_heads * HEAD_DIM:]
    else:
        o_ref[...] = acc


def _norm_matmul(x, g, w, tn, rope=None, n_rope_cols=0, rows_inner=False, cast=(), cast_chunks=32):
    split_x = isinstance(x, tuple)
    k, n = w.shape
    assert n % tn == 0 and (n_rope_cols == 0 or tn == n)
    if split_x:
        in_specs, args = _split_specs(k), list(x)
    else:
        in_specs, args = [pl.BlockSpec((TM, k), lambda i, j: (i, 0))], [x]
    in_specs += [pl.BlockSpec((1, k), lambda i, j: (0, 0)), pl.BlockSpec((k, tn), lambda i, j: (0, j))]
    args += [g.reshape(1, k), w]
    if n_rope_cols:
        in_specs += [_rope_spec()] * 3
        args += list(rope)
    out_specs = [pl.BlockSpec((TM, tn), lambda i, j: (i, j))]
    grid = (M_ALL // TM, n // tn)
    if rows_inner:
        def swapped(spec):
            return pl.BlockSpec(spec.block_shape, lambda a, b, f=spec.index_map: f(b, a))
        in_specs, out_specs, grid = [swapped(s) for s in in_specs], [swapped(s) for s in out_specs], grid[::-1]
    assert not cast or grid[0] * grid[1] >= cast_chunks
    ride = _cast_plumbing(cast, cast_chunks, lambda a, b: a * grid[1] + b)
    outs = pl.pallas_call(
        functools.partial(_norm_matmul_kernel, split_x=split_x, n_rope_heads=n_rope_cols // HEAD_DIM,
                          rows_inner=rows_inner, n_cast=len(cast), cast_chunks=cast_chunks),
        grid=grid,
        in_specs=in_specs + ride.in_specs,
        out_specs=out_specs + ride.out_specs,
        out_shape=[jax.ShapeDtypeStruct((M_ALL, n), F32)] + ride.out_shape,
        scratch_shapes=[pltpu.VMEM((TM, k), BF16)],
        compiler_params=_params(("arbitrary", "arbitrary")),
        name="norm_matmul",
    )(*args, *ride.args)
    return outs if cast else outs[0]


ROW_TILES_PER_SEQ = SEQ // TM


def _kv_proj_kernel(x_ref, g_ref, w_ref, c_ref, s1_ref, s2_ref, *out_refs, head_major):
    h = (_rms_unit(x_ref[...]) * g_ref[...]).astype(BF16)
    acc = jnp.dot(h, w_ref[...], preferred_element_type=F32)
    c, s1, s2 = c_ref[...], s1_ref[...], s2_ref[...]
    for hd in range(N_HEADS):
        k_cols = slice(hd * HEAD_DIM, (hd + 1) * HEAD_DIM)
        v_cols = slice(MIX_WIDTH + hd * HEAD_DIM, MIX_WIDTH + (hd + 1) * HEAD_DIM)
        k_seg = _rotate(acc[:, k_cols], c, s1, s2)
        if head_major:
            kp_ref, vp_ref = out_refs
            kp_ref[hd] = k_seg
            vp_ref[hd] = acc[:, v_cols]
        else:
            (kv_ref,) = out_refs
            kv_ref[:, k_cols] = k_seg
            kv_ref[:, v_cols] = acc[:, v_cols]


def _kv_proj(x, g, w, rope):
    k, n = w.shape
    common = [
        pl.BlockSpec((1, k), lambda i: (0, 0)),
        pl.BlockSpec((k, n), lambda i: (0, 0), pipeline_mode=pl.Buffered(1)),
    ]
    args = (g.reshape(1, k), w) + tuple(rope)
    head_block = pl.BlockSpec((None, N_HEADS, TM, HEAD_DIM),
                              lambda i: (i // ROW_TILES_PER_SEQ, 0, i % ROW_TILES_PER_SEQ, 0))
    head_shape = jax.ShapeDtypeStruct((BATCH, N_HEADS, SEQ, HEAD_DIM), F32)
    kp, vp = pl.pallas_call(
        functools.partial(_kv_proj_kernel, head_major=True),
        grid=(N_PROMPT_TILES,),
        in_specs=[pl.BlockSpec((TM, k), lambda i: (i, 0))] + common
        + [pl.BlockSpec((TM, HEAD_DIM), lambda i: (i % ROW_TILES_PER_SEQ, 0))] * 3,
        out_specs=[head_block, head_block],
        out_shape=[head_shape, head_shape],
        compiler_params=_params(("parallel",)),
        name="kv_proj_prompt",
    )(x, *args)
    kvs = pl.pallas_call(
        functools.partial(_kv_proj_kernel, head_major=False),
        grid=(1,),
        in_specs=[pl.BlockSpec((TM, k), lambda i: (N_PROMPT_TILES, 0))] + common
        + [pl.BlockSpec((TM, HEAD_DIM), lambda i: (ROW_TILES_PER_SEQ, 0))] * 3,
        out_specs=pl.BlockSpec((TM, n), lambda i: (0, 0)),
        out_shape=jax.ShapeDtypeStruct((M_SAMPLE, n), F32),
        compiler_params=_params(("arbitrary",)),
        name="kv_proj_sample",
    )(x, *args)
    return kp, vp, kvs


def _mem_kv_kernel(x_ref, g_ref, w_ref, k_ref, v_ref):
    h = (_rms_unit(x_ref[...]) * g_ref[...]).astype(BF16)
    acc = jnp.dot(h, w_ref[...], preferred_element_type=F32)
    rows = acc.shape[0]
    for hd in range(N_MEM_HEADS):
        k_ref[pl.ds(hd, rows, stride=N_MEM_HEADS), :] = acc[:, hd * HEAD_DIM:(hd + 1) * HEAD_DIM]
        v_ref[pl.ds(hd, rows, stride=N_MEM_HEADS), :] = acc[:, MEM_WIDTH + hd * HEAD_DIM:
                                                             MEM_WIDTH + (hd + 1) * HEAD_DIM]


def _mem_kv(mem_rows, g_mem, w_mem_kv):
    depth = g_mem.shape[0]
    out_block = pl.BlockSpec((None, None, N_MEM * N_MEM_HEADS, HEAD_DIM), lambda l, b: (l, b, 0, 0))
    out_shape = jax.ShapeDtypeStruct((depth, BATCH, N_MEM * N_MEM_HEADS, HEAD_DIM), F32)
    return pl.pallas_call(
        _mem_kv_kernel,
        grid=(depth, BATCH),
        in_specs=[
            pl.BlockSpec((N_MEM, D_MODEL), lambda l, b: (b, 0)),
            pl.BlockSpec((None, 1, D_MODEL), lambda l, b: (l, 0, 0)),
            pl.BlockSpec((None, D_MODEL, 2 * MEM_WIDTH), lambda l, b: (l, 0, 0)),
        ],
        out_specs=[out_block, out_block],
        out_shape=[out_shape, out_shape],
        compiler_params=_params(("parallel", "parallel")),
        name="mem_kv",
    )(mem_rows, g_mem.reshape(depth, 1, D_MODEL), w_mem_kv)


GMLP_CAST_CHUNKS = 64


def _gmlp_kernel(z_ref, gv_ref, wmix_ref, bias_ref, *refs):
    n_cast = (len(refs) - 2) // 2
    src_refs, (mix_ref, v_ref), dst_refs = refs[:n_cast], refs[n_cast:n_cast + 2], refs[n_cast + 2:]
    _ride_casts(src_refs, dst_refs, pl.program_id(0), GMLP_CAST_CHUNKS)
    u = _gelu(z_ref[:, :MIX_WIDTH])
    v = _rms_unit(_gelu(z_ref[:, MIX_WIDTH:2 * MIX_WIDTH])) * gv_ref[...]
    v_ref[...] = v
    vb = v.astype(BF16)
    bias = bias_ref[...]
    for g in range(N_GROUPS):
        lo, hi = g * HEAD_DIM, (g + 1) * HEAD_DIM
        mixed = jnp.dot(wmix_ref[g], vb[:, lo:hi], preferred_element_type=F32)
        mix_ref[:, lo:hi] = (u[:, lo:hi] * (mixed + bias[:, g:g + 1])).astype(BF16)


def _gmlp(z, g_v, wmix, bias, cast=()):
    n_prompt_tiles = M_PROMPT // CHUNK
    assert M_ALL // CHUNK >= GMLP_CAST_CHUNKS
    ride = _cast_plumbing(cast, GMLP_CAST_CHUNKS, lambda i: i)
    return pl.pallas_call(
        _gmlp_kernel,
        grid=(M_ALL // CHUNK,),
        in_specs=[
            pl.BlockSpec((CHUNK, z.shape[1]), lambda i: (i, 0)),
            pl.BlockSpec((1, MIX_WIDTH), lambda i: (0, 0)),
            pl.BlockSpec((None, N_GROUPS, CHUNK, CHUNK), lambda i: (i // n_prompt_tiles, 0, 0, 0)),
            pl.BlockSpec((None, CHUNK, N_GROUPS), lambda i: (i // n_prompt_tiles, 0, 0)),
        ] + ride.in_specs,
        out_specs=[
            pl.BlockSpec((CHUNK, MIX_WIDTH), lambda i: (i, 0)),
            pl.BlockSpec((CHUNK, MIX_WIDTH), lambda i: (jnp.maximum(i - n_prompt_tiles, 0), 0)),
        ] + ride.out_specs,
        out_shape=[
            jax.ShapeDtypeStruct((M_ALL, MIX_WIDTH), BF16),
            jax.ShapeDtypeStruct((M_SAMPLE, MIX_WIDTH), F32),
        ] + ride.out_shape,
        compiler_params=_params(("arbitrary",)),
        name="gmlp",
    )(z, g_v.reshape(1, MIX_WIDTH), wmix, bias, *ride.args)


def _mem_attend_one(q, k_ref, v_ref, o_ref, rows):
    for h in range(N_MEM_HEADS):
        cols = slice(h * HEAD_DIM, (h + 1) * HEAD_DIM)
        k = k_ref[pl.ds(h, N_MEM, stride=N_MEM_HEADS), :].astype(BF16)
        v = v_ref[pl.ds(h, N_MEM, stride=N_MEM_HEADS), :].astype(BF16)
        s = lax.dot_general(q[:, cols].astype(BF16), k, _NT, preferred_element_type=F32) * SCALE
        p = jnp.exp(s - jnp.max(s, axis=-1, keepdims=True))
        l = jnp.sum(p, axis=-1, keepdims=True)
        o = jnp.dot(p.astype(BF16), v, preferred_element_type=F32)
        o_ref[rows, cols] = (o / l).astype(o_ref.dtype)


def _mem_prompt_kernel(q_ref, k_ref, v_ref, o_ref):
    _mem_attend_one(q_ref[...], k_ref, v_ref, o_ref, slice(None))


def _mem_attend_prompt(z, q_col_block, mem_k, mem_v, layer, tq=512):
    nq = SEQ // tq
    kv_spec = pl.BlockSpec((None, None, N_MEM * N_MEM_HEADS, HEAD_DIM), lambda b, i: (layer, b, 0, 0))
    return pl.pallas_call(
        _mem_prompt_kernel,
        grid=(BATCH, nq),
        in_specs=[pl.BlockSpec((tq, MEM_WIDTH), lambda b, i: (b * nq + i, q_col_block)), kv_spec, kv_spec],
        out_specs=pl.BlockSpec((tq, MEM_WIDTH), lambda b, i: (b * nq + i, 0)),
        out_shape=jax.ShapeDtypeStruct((M_PROMPT, MEM_WIDTH), BF16),
        compiler_params=_params(("parallel", "parallel")),
        name="mem_attend_prompt",
    )(z, mem_k, mem_v)


def _mem_sample_kernel(q_ref, k_ref, v_ref, o_ref, *, n_seq):
    n_rows, n_cols = N_MEM_HEADS * T_PAD, N_MEM * N_MEM_HEADS
    same_head = (lax.broadcasted_iota(jnp.int32, (n_rows, n_cols), 1) % N_MEM_HEADS
                 == lax.broadcasted_iota(jnp.int32, (n_rows, n_cols), 0) // T_PAD)
    scores = []
    for s in range(n_seq):
        q8 = q_ref[s]
        q_all = jnp.concatenate([q8[:, h * HEAD_DIM:(h + 1) * HEAD_DIM] for h in range(N_MEM_HEADS)], axis=0)
        sc = lax.dot_general(q_all.astype(BF16), k_ref[s].astype(BF16), _NT,
                             preferred_element_type=F32) * SCALE
        scores.append(jnp.where(same_head, sc, -jnp.inf))
    probs = []
    for sc in scores:
        p = jnp.exp(sc - jnp.max(sc, axis=-1, keepdims=True))
        probs.append((p.astype(BF16), jnp.sum(p, axis=-1, keepdims=True)))
    for s, (p, l) in enumerate(probs):
        o = jnp.dot(p, v_ref[s].astype(BF16), preferred_element_type=F32) / l
        for h in range(N_MEM_HEADS):
            o_ref[s, :, h * HEAD_DIM:(h + 1) * HEAD_DIM] = o[h * T_PAD:(h + 1) * T_PAD]


def _mem_attend_sample(q8, mem_k, mem_v, layer, n_seq=8):
    kv_spec = pl.BlockSpec((None, n_seq, N_MEM * N_MEM_HEADS, HEAD_DIM), lambda i: (layer, i, 0, 0))
    tok_spec = pl.BlockSpec((n_seq, T_PAD, MEM_WIDTH), lambda i: (i, 0, 0))
    return pl.pallas_call(
        functools.partial(_mem_sample_kernel, n_seq=n_seq),
        grid=(DEC_BATCH // n_seq,),
        in_specs=[tok_spec, kv_spec, kv_spec],
        out_specs=tok_spec,
        out_shape=jax.ShapeDtypeStruct((DEC_BATCH, T_PAD, MEM_WIDTH), F32),
        compiler_params=_params(("parallel",)),
        name="mem_attend_sample",
    )(q8, mem_k, mem_v)


def _out_proj_kernel(*refs, split_x):
    if split_x:
        x_ref, xs_ref, mixp_ref, mixs_ref, mop_ref, mos_ref, w1_ref, w2_ref, o_ref = refs
    else:
        x_ref, mixp_ref, mixs_ref, mop_ref, mos_ref, w1_ref, w2_ref, o_ref = refs
        xs_ref = x_ref
    i = pl.program_id(0)

    def run(x_r, mix_r, mo_r):
        acc = jnp.dot(mix_r[...].astype(BF16), w1_ref[...], preferred_element_type=F32)
        acc += jnp.dot(mo_r[...].astype(BF16), w2_ref[...], preferred_element_type=F32)
        o_ref[...] = x_r[...] + acc

    pl.when(i < N_PROMPT_TILES)(lambda: run(x_ref, mixp_ref, mop_ref))
    pl.when(i >= N_PROMPT_TILES)(lambda: run(xs_ref, mixs_ref, mos_ref))


def _out_proj(x, mix, mo, w_out, tn=D_MODEL):
    split_x = isinstance(x, tuple)
    if split_x:
        in_specs, args = _split_specs(tn, lambda j: j), list(x)
    else:
        in_specs, args = [pl.BlockSpec((TM, tn), lambda i, j: (i, j))], [x]
    in_specs += _split_specs(MIX_WIDTH) + _split_specs(MEM_WIDTH)
    in_specs += [
        pl.BlockSpec((MIX_WIDTH, tn), lambda i, j: (0, j)),
        pl.BlockSpec((MEM_WIDTH, tn), lambda i, j: (MIX_WIDTH // MEM_WIDTH, j)),
    ]
    args += [mix[0], mix[1], mo[0], mo[1], w_out, w_out]
    return pl.pallas_call(
        functools.partial(_out_proj_kernel, split_x=split_x),
        grid=(M_ALL // TM, D_MODEL // tn),
        in_specs=in_specs,
        out_specs=pl.BlockSpec((TM, tn), lambda i, j: (i, j)),
        out_shape=jax.ShapeDtypeStruct((M_ALL, D_MODEL), F32),
        compiler_params=_params(("parallel", "parallel")),
        name="out_proj",
    )(*args)


def _mlp_kernel(*refs, final_norm, n_cast, cast_chunks):
    refs = list(refs)
    x_ref, g_ref, wu_ref, wd_ref = refs[:4]
    del refs[:4]
    if final_norm:
        gf_ref = refs.pop(0)
    src_refs = refs[:n_cast]
    del refs[:n_cast]
    if final_norm:
        yp_ref, ys_ref = refs[:2]
        del refs[:2]
    else:
        o_ref = refs.pop(0)
    dst_refs = refs[:n_cast]
    del refs[:n_cast]
    h_ref = refs.pop(0)
    acc_ref = refs.pop(0) if final_norm else o_ref
    i, f = pl.program_id(0), pl.program_id(1)
    last = f == pl.num_programs(1) - 1

    _ride_casts(src_refs, dst_refs, i * pl.num_programs(1) + f, cast_chunks)

    @pl.when(f == 0)
    def _():
        x = x_ref[...]
        h_ref[...] = (_rms_unit(x) * g_ref[...]).astype(BF16)
        acc_ref[...] = jnp.zeros_like(acc_ref) if final_norm else x

    a = jnp.maximum(jnp.dot(h_ref[...], wu_ref[...], preferred_element_type=F32), 0.0)
    acc_ref[...] += jnp.dot((a * a).astype(BF16), wd_ref[...], preferred_element_type=F32)

    if final_norm:
        def finish(ref):
            ref[...] = _rms_unit(x_ref[...] + acc_ref[...]) * gf_ref[...]

        pl.when(last & (i < N_PROMPT_TILES))(lambda: finish(yp_ref))
        pl.when(last & (i >= N_PROMPT_TILES))(lambda: finish(ys_ref))


def _mlp(x, g, w_up, w_down, g_final=None, cast=(), tf=1024):
    d, dff = w_up.shape
    nf = dff // tf
    final_norm = g_final is not None
    in_specs = [
        pl.BlockSpec((TM, d), lambda i, f: (i, 0)),
        pl.BlockSpec((1, d), lambda i, f: (0, 0)),
        pl.BlockSpec((d, tf), lambda i, f: (0, f)),
        pl.BlockSpec((tf, d), lambda i, f: (f, 0)),
    ]
    args = [x, g.reshape(1, d), w_up, w_down]
    if final_norm:
        in_specs.append(pl.BlockSpec((1, d), lambda i, f: (0, 0)))
        args.append(g_final.reshape(1, d))
        out_specs = [
            pl.BlockSpec((TM, d), lambda i, f: (jnp.minimum(i, N_PROMPT_TILES - 1), 0)),
            pl.BlockSpec((TM, d), lambda i, f: (0, 0)),
        ]
        out_shape = [jax.ShapeDtypeStruct((M_PROMPT, d), F32), jax.ShapeDtypeStruct((M_SAMPLE, d), F32)]
    else:
        out_specs = [pl.BlockSpec((TM, d), lambda i, f: (i, 0))]
        out_shape = [jax.ShapeDtypeStruct((M_ALL, d), F32)]

    cast_chunks = 128
    assert (M_ALL // TM) * nf >= cast_chunks
    ride = _cast_plumbing(cast, cast_chunks, lambda i, f: i * nf + f)
    scratch = [pltpu.VMEM((TM, d), BF16)] + ([pltpu.VMEM((TM, d), F32)] if final_norm else [])
    return pl.pallas_call(
        functools.partial(_mlp_kernel, final_norm=final_norm, n_cast=len(cast), cast_chunks=cast_chunks),
        grid=(M_ALL // TM, nf),
        in_specs=in_specs + ride.in_specs,
        out_specs=out_specs + ride.out_specs,
        out_shape=out_shape + ride.out_shape,
        scratch_shapes=scratch,
        compiler_params=_params(("arbitrary", "arbitrary")),
        name="mlp",
    )(*args, *ride.args)


def _prompt_setup(q_ref, k_ref, v_ref, qa_ref, ka_ref, va_ref):
    lane = lax.broadcasted_iota(jnp.int32, (SEQ, HEAD_DIM), 1)
    row_blk = lax.broadcasted_iota(jnp.int32, (SEQ, HEAD_DIM), 0) // MOBA_BLOCK
    k = k_ref[...]
    q = q_ref[...]

    ka_ref[:, :HEAD_DIM] = k.astype(BF16)
    ka_ref[:, HEAD_DIM:] = (lane == row_blk).astype(BF16)
    va_ref[:, :HEAD_DIM] = v_ref[...].astype(BF16)
    va_ref[:, HEAD_DIM:] = (lane == 0).astype(BF16)

    kmean = jnp.mean(k.reshape(N_BLOCKS, MOBA_BLOCK, HEAD_DIM), axis=1)
    gate = _dot_nt_3pass(kmean, q)
    blk = lax.broadcasted_iota(jnp.int32, gate.shape, 0)
    q_blk = lax.broadcasted_iota(jnp.int32, gate.shape, 1) // MOBA_BLOCK
    sel = _top3_mask(gate, blk < q_blk, axis=0)
    eye = (lax.broadcasted_iota(jnp.int32, (N_BLOCKS, HEAD_DIM), 0)
           == lax.broadcasted_iota(jnp.int32, (N_BLOCKS, HEAD_DIM), 1)).astype(BF16)
    sel_cols = lax.dot_general(sel.astype(BF16), eye, _TN, preferred_element_type=F32)
    qa_ref[:, :HEAD_DIM] = (q * EXP2_SCALE).astype(BF16)
    qa_ref[:, HEAD_DIM:] = jnp.where(sel_cols > 0.5, 0.0, MASKED).astype(BF16)


def _prompt_tile(qi, o_ref, qa_ref, ka_ref, va_ref):
    causal = (lax.broadcasted_iota(jnp.int32, (MOBA_BLOCK, MOBA_BLOCK), 1)
              <= lax.broadcasted_iota(jnp.int32, (MOBA_BLOCK, MOBA_BLOCK), 0))
    rows = slice(qi * MOBA_BLOCK, (qi + 1) * MOBA_BLOCK)
    past = slice(0, qi * MOBA_BLOCK)
    s_own = lax.dot_general(qa_ref[rows, :HEAD_DIM], ka_ref[rows, :HEAD_DIM], _NT, preferred_element_type=F32)
    s_own = jnp.where(causal, s_own, MASKED)
    m = jnp.max(s_own, axis=-1, keepdims=True)
    if qi:
        s_past = lax.dot_general(qa_ref[rows, :], ka_ref[past, :], _NT, preferred_element_type=F32)
        m = jnp.maximum(m, jnp.max(s_past, axis=-1, keepdims=True))
    pv = jnp.dot(jnp.exp2(s_own - m).astype(BF16), va_ref[rows, :], preferred_element_type=F32)
    if qi:
        pv += jnp.dot(jnp.exp2(s_past - m).astype(BF16), va_ref[past, :], preferred_element_type=F32)
    o_ref[rows, :] = (pv[:, :HEAD_DIM] / pv[:, HEAD_DIM:HEAD_DIM + 1]).astype(BF16)


PROMPT_SUBSTEPS = ((0,), (1, 2, 3), (4, 5), (6,), (7,))
N_PROMPT_UNITS = BATCH * N_HEADS
assert sorted(qi for tiles in PROMPT_SUBSTEPS for qi in tiles) == list(range(N_BLOCKS))


def _head_diag(x):
    return jnp.concatenate(
        [x[h * T_PAD:(h + 1) * T_PAD, h * HEAD_DIM:(h + 1) * HEAD_DIM] for h in range(N_HEADS)], axis=1)


BLOCKS_PER_STEP = 4
PAGES_PER_STEP = BLOCKS_PER_STEP * PAGES_PER_BLOCK
STEPS_PER_SEQ = N_PAST_BLOCKS // BLOCKS_PER_STEP
assert N_PAST_BLOCKS % BLOCKS_PER_STEP == 0


def _moba_kernel(pt_ref, q_ref, kn_ref, vn_ref, hm_ref, *refs):
    del pt_ref
    k_refs, v_refs = refs[:PAGES_PER_STEP], refs[PAGES_PER_STEP:2 * PAGES_PER_STEP]
    (pq_ref, pk_ref, pv_ref, o_ref, po_ref,
     qf_ref, qb_ref, kmean_ref, m_ref, l_ref, part_ref, qa_ref, ka_ref, va_ref) = refs[2 * PAGES_PER_STEP:]
    step = pl.program_id(1)
    part_row = lax.broadcasted_iota(jnp.int32, (N_PART, QCOLS), 0)

    t = pl.program_id(0) * STEPS_PER_SEQ + step
    prompt_active = t < N_PROMPT_UNITS * len(PROMPT_SUBSTEPS)
    for sub, tiles in enumerate(PROMPT_SUBSTEPS):
        @pl.when(prompt_active & (t % len(PROMPT_SUBSTEPS) == sub))
        def _(sub=sub, tiles=tiles):
            if sub == 0:
                _prompt_setup(pq_ref, pk_ref, pv_ref, qa_ref, ka_ref, va_ref)
            for qi in tiles:
                _prompt_tile(qi, po_ref, qa_ref, ka_ref, va_ref)

    @pl.when(step == 0)
    def _():
        q8 = q_ref[...]
        qrep = jnp.broadcast_to(q8[None], (QCOLS // T_PAD, T_PAD, MIX_WIDTH)).reshape(QCOLS, MIX_WIDTH)
        qbd = qrep * hm_ref[...]
        qf_ref[...] = qbd
        qb_ref[...] = qbd.astype(BF16)
        kmean_ref[...] = jnp.zeros_like(kmean_ref)
        m_ref[...] = jnp.zeros_like(m_ref)
        l_ref[...] = jnp.zeros_like(l_ref)

    qb = qb_ref[...]

    def scores_t(k_bf):
        return lax.dot_general(k_bf, qb, _NT, preferred_element_type=F32) * SCALE

    def load_block(page_refs, b, with_sum):
        rows, total = [], None
        for r in page_refs[b * PAGES_PER_BLOCK:(b + 1) * PAGES_PER_BLOCK]:
            slabs = [r[h] for h in range(N_HEADS)]
            rows.append(jnp.concatenate([x.astype(BF16) for x in slabs], axis=1))
            if with_sum:
                part = jnp.concatenate([jnp.sum(x, axis=0, keepdims=True) for x in slabs], axis=1)
                total = part if total is None else total + part
        return jnp.concatenate(rows, axis=0), total

    k_blocks = [load_block(k_refs, b, True) for b in range(BLOCKS_PER_STEP)]
    s_all = scores_t(jnp.concatenate([kb for kb, _ in k_blocks], axis=0))
    mean_row = lax.broadcasted_iota(jnp.int32, kmean_ref.shape, 0)
    m_new, l_new, kmean_new = m_ref[...], l_ref[...], kmean_ref[...]
    for b in range(BLOCKS_PER_STEP):
        n = step * BLOCKS_PER_STEP + b
        s = s_all[b * MOBA_BLOCK:(b + 1) * MOBA_BLOCK]
        mn = jnp.max(s, axis=0, keepdims=True)
        p = jnp.exp(s - mn)
        ln = jnp.sum(p, axis=0, keepdims=True)
        pv = jnp.dot(p.T.astype(BF16), load_block(v_refs, b, False)[0], preferred_element_type=F32)
        part_ref[n] = _head_diag(pv)
        m_new = jnp.where(part_row == n, mn, m_new)
        l_new = jnp.where(part_row == n, ln, l_new)
        kmean_new = jnp.where(mean_row == n, k_blocks[b][1] * (1.0 / MOBA_BLOCK), kmean_new)
    m_ref[...] = m_new
    l_ref[...] = l_new
    kmean_ref[...] = kmean_new

    @pl.when(step == STEPS_PER_SEQ - 1)
    def _():
        tail = jnp.zeros((128 - T_PAD, MIX_WIDTH), F32)
        knb = jnp.concatenate([kn_ref[...], tail], axis=0).astype(BF16)
        vnb = jnp.concatenate([vn_ref[...], tail], axis=0).astype(BF16)
        sc = scores_t(knb)
        key_t = lax.broadcasted_iota(jnp.int32, sc.shape, 0)
        qry_t = lax.broadcasted_iota(jnp.int32, sc.shape, 1) % T_PAD
        sc = jnp.where(key_t <= jnp.minimum(qry_t, DEC_SEQ - 1), sc, -jnp.inf)
        mc = jnp.max(sc, axis=0, keepdims=True)
        pc = jnp.exp(sc - mc)
        lc = jnp.sum(pc, axis=0, keepdims=True)
        oc = _head_diag(jnp.dot(pc.T.astype(BF16), vnb, preferred_element_type=F32))
        m_all = jnp.where(part_row == N_PAST_BLOCKS, mc, m_ref[...])
        l_all = jnp.where(part_row == N_PAST_BLOCKS, lc, l_ref[...])

        gate = _dot_nt_3pass(kmean_ref[...], qf_ref[...])
        sel = _top3_mask(gate, part_row < N_PAST_BLOCKS, axis=0)
        sel = jnp.where(part_row == N_PAST_BLOCKS, 1.0, sel)
        m_tot = jnp.max(jnp.where(sel > 0.5, m_all, -jnp.inf), axis=0, keepdims=True)
        w = jnp.where(sel > 0.5, jnp.exp(m_all - m_tot), 0.0)
        w = w * (1.0 / jnp.sum(w * l_all, axis=0, keepdims=True))
        w_cols = jnp.concatenate([w, jnp.zeros((QCOLS - N_PART, QCOLS), F32)], axis=0).T
        for h in range(N_HEADS):
            rows = slice(h * T_PAD, (h + 1) * T_PAD)
            cols = slice(h * HEAD_DIM, (h + 1) * HEAD_DIM)
            acc = w_cols[rows, N_PAST_BLOCKS:N_PAST_BLOCKS + 1] * oc[:, cols]
            for b in range(N_PAST_BLOCKS):
                acc += w_cols[rows, b:b + 1] * part_ref[b, :, cols]
            o_ref[:, cols] = acc


def _moba(zq, kp, vp, q8, kn8, vn8, cache_k, cache_v, pt_flat):
    n_sub = len(PROMPT_SUBSTEPS)
    assert DEC_BATCH * STEPS_PER_SEQ >= N_PROMPT_UNITS * n_sub

    def unit(s, n):
        return jnp.minimum((s * STEPS_PER_SEQ + n) // n_sub, N_PROMPT_UNITS - 1)

    prompt_rows = pl.BlockSpec((SEQ, HEAD_DIM), lambda s, n, pt: (unit(s, n) // N_HEADS, unit(s, n) % N_HEADS))
    prompt_kv = pl.BlockSpec((None, None, SEQ, HEAD_DIM),
                             lambda s, n, pt: (unit(s, n) // N_HEADS, unit(s, n) % N_HEADS, 0, 0))
    head_of_col = jnp.arange(MIX_WIDTH, dtype=jnp.int32) // HEAD_DIM
    head_of_row = jnp.arange(QCOLS, dtype=jnp.int32) // T_PAD
    head_mask = (head_of_row[:, None] == head_of_col[None, :]).astype(F32)

    def tok_spec():
        return pl.BlockSpec((None, T_PAD, MIX_WIDTH), lambda s, n, pt: (s, 0, 0))

    def page_spec(j):
        return pl.BlockSpec((None, N_HEADS, PAGE_SIZE, HEAD_DIM),
                            lambda s, n, pt: (pt[s * N_PAGES + n * PAGES_PER_STEP + j], 0, 0, 0))

    page_specs = [page_spec(j) for j in range(PAGES_PER_STEP)]
    grid_spec = pltpu.PrefetchScalarGridSpec(
        num_scalar_prefetch=1,
        grid=(DEC_BATCH, STEPS_PER_SEQ),
        in_specs=[
            tok_spec(), tok_spec(), tok_spec(),
            pl.BlockSpec((QCOLS, MIX_WIDTH), lambda s, n, pt: (0, 0)),
        ] + page_specs + page_specs + [prompt_rows, prompt_kv, prompt_kv],
        out_specs=[pl.BlockSpec((None, T_PAD, MIX_WIDTH), lambda s, n, pt: (s, 0, 0)), prompt_rows],
        scratch_shapes=[
            pltpu.VMEM((QCOLS, MIX_WIDTH), F32),
            pltpu.VMEM((QCOLS, MIX_WIDTH), BF16),
            pltpu.VMEM((N_PART, MIX_WIDTH), F32),
            pltpu.VMEM((N_PART, QCOLS), F32),
            pltpu.VMEM((N_PART, QCOLS), F32),
            pltpu.VMEM((N_PAST_BLOCKS, T_PAD, MIX_WIDTH), F32),
        ] + [pltpu.VMEM((SEQ, 2 * HEAD_DIM), BF16)] * 3,
    )
    moba_s, moba_p = pl.pallas_call(
        _moba_kernel,
        grid_spec=grid_spec,
        out_shape=[jax.ShapeDtypeStruct((DEC_BATCH, T_PAD, MIX_WIDTH), F32),
                   jax.ShapeDtypeStruct((M_PROMPT, MIX_WIDTH), BF16)],
        compiler_params=_params(("arbitrary", "arbitrary")),
        name="moba",
    )(pt_flat, q8, kn8, vn8, head_mask, *([cache_k] * PAGES_PER_STEP), *([cache_v] * PAGES_PER_STEP),
      zq, kp, vp)
    return moba_p, moba_s


def _rope_tables():
    half = ROT_DIM // 2
    inv = ROPE_THETA ** (-(np.arange(half, dtype=np.float64) * 2.0) / ROT_DIM)
    pos = np.concatenate([np.arange(SEQ), PAST_LEN + np.arange(TM) % DEC_SEQ]).astype(np.float64)
    ang = pos[:, None] * inv[None, :]
    cos, sin = np.cos(ang), np.sin(ang)
    zeros = np.zeros_like(cos)
    pad = np.zeros((pos.shape[0], HEAD_DIM - ROT_DIM))
    c = np.concatenate([cos, cos, pad + 1.0], axis=1)
    s1 = np.concatenate([-sin, zeros, pad], axis=1)
    s2 = np.concatenate([zeros, sin, pad], axis=1)
    return tuple(jnp.asarray(t, dtype=F32) for t in (c, s1, s2))


def _pad_tokens(rows):
    x = rows.reshape(DEC_BATCH, DEC_SEQ, rows.shape[-1])
    return jnp.pad(x, ((0, 0), (0, T_PAD - DEC_SEQ), (0, 0)))


def _unpad_tokens(x):
    return x[:, :DEC_SEQ].reshape(M_SAMPLE, x.shape[-1])


def kernel(x_prompt, x_sample, cache_k, cache_v, cache_mem_k, cache_mem_v, page_table, mem_prompt,
           g_mix, w_in_a, w_in_b, g_v, w_s, b_s, w_out, g_mlp, w_up, w_down, g_mem, w_mem_kv,
           g_kv, w_kv, g_final):
    depth = g_mix.shape[0]
    assert depth == 2 and w_in_a.shape[0] == 1 and w_in_b.shape[0] == 1

    x0 = (x_prompt.reshape(M_PROMPT, D_MODEL), x_sample.reshape(M_SAMPLE, D_MODEL))
    rope = _rope_tables()

    mem_rows = N_MEM * N_MEM_HEADS
    mem_k_p, mem_v_p = _mem_kv(mem_prompt.reshape(BATCH * N_MEM, D_MODEL), g_mem, w_mem_kv.astype(BF16))
    mem_k_s = cache_mem_k.reshape(depth, DEC_BATCH, mem_rows, HEAD_DIM)
    mem_v_s = cache_mem_v.reshape(depth, DEC_BATCH, mem_rows, HEAD_DIM)

    n_in_a = 2 * MIX_WIDTH + MEM_WIDTH
    z, w_up0 = _norm_matmul(x0, g_mix[0], w_in_a[0].astype(BF16), tn=n_in_a // 2,
                            rows_inner=True, cast=((w_up, 0),))
    tril = jnp.tril(jnp.ones((CHUNK, CHUNK), bool))
    wmix_p = jnp.where(tril[None], w_s[0], 0.0)
    w_small = jnp.where(tril[None, :DEC_SEQ, :DEC_SEQ], w_s[0][:, :DEC_SEQ, :DEC_SEQ], 0.0)
    n_rep = CHUNK // DEC_SEQ
    wmix_s = jnp.einsum('ab,gts->gatbs', jnp.eye(n_rep, dtype=F32), w_small).reshape(N_GROUPS, CHUNK, CHUNK)
    wmix = jnp.stack([wmix_p, wmix_s]).astype(BF16)
    bias = jnp.stack([b_s[0].T, jnp.tile(b_s[0][:, :DEC_SEQ].T, (n_rep, 1))])
    mix, v_rows, w_down0, w_out0 = _gmlp(z, g_v[0], wmix, bias, cast=((w_down, 0), (w_out, 0)))
    q_blk = 2 * MIX_WIDTH // MEM_WIDTH
    mo = (_mem_attend_prompt(z, q_blk, mem_k_p, mem_v_p, 0),
          _unpad_tokens(_mem_attend_sample(_pad_tokens(z[M_PROMPT:, 2 * MIX_WIDTH:]), mem_k_s, mem_v_s, 0)))
    x = _out_proj(x0, (mix, mix[M_PROMPT:]), mo, w_out0)
    x, w_up1, w_down1, w_out1, w_kv_b, w_in_b1 = _mlp(
        x, g_mlp[0], w_up0, w_down0,
        cast=((w_up, 1), (w_down, 1), (w_out, 1), (w_kv, None), (w_in_b, 0)))

    kp, vp, kvs = _kv_proj(x, g_kv, w_kv_b, rope)
    zq = _norm_matmul(x, g_mix[1], w_in_b1, tn=D_MODEL, rope=rope, n_rope_cols=MIX_WIDTH)
    moba_p, moba_s = _moba(
        zq, kp, vp,
        _pad_tokens(zq[M_PROMPT:, :MIX_WIDTH]),
        _pad_tokens(kvs[:, :MIX_WIDTH]),
        _pad_tokens(kvs[:, MIX_WIDTH:]),
        jnp.transpose(cache_k, (0, 2, 1, 3)),
        jnp.transpose(cache_v, (0, 2, 1, 3)),
        page_table.reshape(-1),
    )
    moba_s = _unpad_tokens(moba_s)
    q_blk = MIX_WIDTH // MEM_WIDTH
    mo = (_mem_attend_prompt(zq, q_blk, mem_k_p, mem_v_p, 1),
          _unpad_tokens(_mem_attend_sample(_pad_tokens(zq[M_PROMPT:, MIX_WIDTH:]), mem_k_s, mem_v_s, 1)))
    x = _out_proj(x, (moba_p, moba_s), mo, w_out1)
    y_p, y_s = _mlp(x, g_mlp[1], w_up1, w_down1, g_final=g_final)

    y_prompt = y_p.reshape(BATCH, SEQ, D_MODEL)
    y_sample = y_s.reshape(DEC_BATCH, DEC_SEQ, D_MODEL)
    k_prompt = jnp.transpose(kp, (0, 2, 1, 3))
    v_prompt = jnp.transpose(vp, (0, 2, 1, 3))
    k_sample = kvs[:, :MIX_WIDTH].reshape(DEC_BATCH, DEC_SEQ, N_HEADS, HEAD_DIM)
    v_sample = kvs[:, MIX_WIDTH:].reshape(DEC_BATCH, DEC_SEQ, N_HEADS, HEAD_DIM)
    mem_shape = (depth, BATCH, N_MEM, N_MEM_HEADS, HEAD_DIM)
    gmlp_v_sample = v_rows.reshape(1, DEC_BATCH, DEC_SEQ, MIX_WIDTH)
    return (y_prompt, y_sample, k_prompt, v_prompt, k_sample, v_sample,
            mem_k_p.reshape(mem_shape), mem_v_p.reshape(mem_shape), gmlp_v_sample)
```
